```python
import jax, jax.numpy as jnp
from jax import lax
import numpy as np

D_MODEL = 1024
BATCH = 4
SEQ = 4096
DEPTH = 2
DEC_BATCH = 32
DEC_SEQ = 4
PAST_LEN = 16384
PAGE_SIZE = 128

HEAD_DIM = 64
N_SB_HEADS = 8
N_MOBA_HEADS = 8
N_ATT_HEADS = N_SB_HEADS + N_MOBA_HEADS
ATT_WIDTH = N_ATT_HEADS * HEAD_DIM
ATT_SCALE = HEAD_DIM ** -0.5
SB_Q_BLOCK = 128
MOBA_BLOCK = 256
MOBA_TOPK = 3
MOBA_Q_CHUNK = 32
POOL_WINDOWS = (2, 4, 8, 16)
N_POOL_GROUPS = len(POOL_WINDOWS)
POOL_GROUP_DIM = D_MODEL // N_POOL_GROUPS
POOL_STATE = max(POOL_WINDOWS) - 1
N_EXPERT_GROUPS = 4
EXPERTS_PER_GROUP = 8
N_EXPERTS = N_EXPERT_GROUPS * EXPERTS_PER_GROUP
EXPERT_TOPK = 2
EXPERT_FF = D_MODEL // 4
N_ATT_LAYERS = (DEPTH + 1) // 2
N_POOL_LAYERS = DEPTH // 2
RMS_EPS = 1e-6

kernel_name = "hybrid_stickbreak_moba_pool_hmoe_step"


def rms_norm(x, g):
    xf = x.astype(jnp.float32)
    y = xf * lax.rsqrt(jnp.mean(xf * xf, axis=-1, keepdims=True) + RMS_EPS)
    return (y * g.astype(jnp.float32)).astype(x.dtype)


def query_sweep(fn, q, q_pos, block):
    B, T = q.shape[:2]
    if T <= block:
        return fn(q, q_pos)
    n = T // block
    qb = jnp.moveaxis(q.reshape(B, n, block, *q.shape[2:]), 1, 0)
    out = lax.map(lambda a: fn(a[0], a[1]), (qb, q_pos.reshape(n, block)))
    return jnp.moveaxis(out, 0, 1).reshape(q.shape[:2] + out.shape[3:])


def stick_breaking_block(q, k, v, q_pos, k_pos):
    z = jnp.einsum("bqhd,bkhd->bhqk", q, k).astype(jnp.float32) * ATT_SCALE
    causal = (k_pos[None, :] < q_pos[:, None])[None, None]
    log_keep = jnp.where(causal, jax.nn.log_sigmoid(-z), 0.0)
    log_between = lax.cumsum(log_keep, axis=3, reverse=True) - log_keep
    w = jnp.where(causal, jnp.exp(jax.nn.log_sigmoid(z) + log_between), 0.0)
    return jnp.einsum("bhqk,bkhd->bqhd", w.astype(v.dtype), v)


def to_blocks(x):
    B, T = x.shape[:2]
    nb = -(-T // MOBA_BLOCK)
    x = jnp.pad(x, ((0, 0), (0, nb * MOBA_BLOCK - T), (0, 0), (0, 0)))
    return x.reshape(B, nb, MOBA_BLOCK, *x.shape[2:])


def moba_chunk(q, q_pos, k_blk, v_blk, k_mean):
    B, Tq, H, _ = q.shape
    NB = k_blk.shape[1]
    n_sel = min(MOBA_TOPK, NB)
    q_blk = q_pos // MOBA_BLOCK
    gate = jnp.einsum("bqhd,bnhd->bhqn", q, k_mean).astype(jnp.float32)
    fully_past = jnp.arange(NB)[None, :] < q_blk[:, None]
    gate = jnp.where(fully_past[None, None], gate, -jnp.inf)
    top_val, top_idx = lax.top_k(gate, n_sel)
    sel_ok = top_val > -jnp.inf
    own = jnp.broadcast_to(q_blk[None, None, :, None], (B, H, Tq, 1))
    idx = jnp.concatenate([top_idx, own], axis=-1)
    bi = jnp.arange(B)[:, None, None, None]
    hi = jnp.arange(H)[None, :, None, None]
    k_sel = k_blk[bi, idx, :, hi]
    v_sel = v_blk[bi, idx, :, hi]
    s = jnp.einsum("bqhd,bhqnkd->bhqnk", q, k_sel).astype(jnp.float32) * ATT_SCALE
    key_pos = idx[..., None] * MOBA_BLOCK + jnp.arange(MOBA_BLOCK)
    block_ok = jnp.concatenate([sel_ok, jnp.ones_like(own, dtype=bool)], axis=-1)[..., None]
    valid = (key_pos <= q_pos[None, None, :, None, None]) & block_ok
    s = jnp.where(valid, s, -jnp.inf)
    p = jax.nn.softmax(s.reshape(B, H, Tq, -1), axis=-1).reshape(s.shape)
    return jnp.einsum("bhqnk,bhqnkd->bqhd", p.astype(v_sel.dtype), v_sel)


def ab_mixer(h, past_k, past_v, pos0, w_qkv, q_g, k_g, w_o):
    B, T, _ = h.shape
    qkv = (h @ w_qkv).reshape(B, T, 3, N_ATT_HEADS, HEAD_DIM)
    q = rms_norm(qkv[:, :, 0], q_g)
    k = rms_norm(qkv[:, :, 1], k_g)
    v = qkv[:, :, 2]
    k_all = k if past_k is None else jnp.concatenate([past_k, k], axis=1)
    v_all = v if past_v is None else jnp.concatenate([past_v, v], axis=1)
    k_pos = jnp.arange(k_all.shape[1])
    q_pos = pos0 + jnp.arange(T)
    sb = slice(0, N_SB_HEADS)
    mb = slice(N_SB_HEADS, N_ATT_HEADS)
    k_sb, v_sb = k_all[:, :, sb], v_all[:, :, sb]
    o_sb = query_sweep(lambda qq, pp: stick_breaking_block(qq, k_sb, v_sb, pp, k_pos),
                       q[:, :, sb], q_pos, SB_Q_BLOCK)
    k_blk = to_blocks(k_all[:, :, mb])
    v_blk = to_blocks(v_all[:, :, mb])
    k_mean = jnp.mean(k_blk.astype(jnp.float32), axis=2).astype(k_blk.dtype)
    o_mb = query_sweep(lambda qq, pp: moba_chunk(qq, pp, k_blk, v_blk, k_mean),
                       q[:, :, mb], q_pos, MOBA_Q_CHUNK)
    o = jnp.concatenate([o_sb, o_mb], axis=2).reshape(B, T, ATT_WIDTH)
    return o @ w_o, k, v


def pool_mixer(h, prefix, pos0, w_pool, scale):
    B, T, D = h.shape
    P = prefix.shape[1]
    full = jnp.concatenate([prefix, h], axis=1)
    csum = jnp.pad(lax.cumsum(full.astype(jnp.float32), axis=1), ((0, 0), (1, 0), (0, 0)))
    pos = pos0 + jnp.arange(T)
    hf = h.astype(jnp.float32)
    diffs = []
    for g, w in enumerate(POOL_WINDOWS):
        c = slice(g * POOL_GROUP_DIM, (g + 1) * POOL_GROUP_DIM)
        win_sum = csum[:, P + 1:P + T + 1, c] - csum[:, P + 1 - w:P + T + 1 - w, c]
        count = jnp.minimum(w, pos + 1).astype(jnp.float32)[None, :, None]
        diffs.append(win_sum / count - hf[:, :, c])
    d = jnp.stack(diffs, axis=2)
    y = jnp.einsum("btgc,gce->btge", d, w_pool.astype(jnp.float32)).reshape(B, T, D)
    return (y * scale.astype(jnp.float32)).astype(h.dtype), full[:, -POOL_STATE:]


def hier_moe(x, w_router_group, w_router_expert, w_gate, w_up, w_down):
    B, T, D = x.shape
    xt = x.reshape(B * T, D)
    g_prob = jax.nn.softmax((xt @ w_router_group).astype(jnp.float32), axis=-1)
    g_val, g_idx = lax.top_k(g_prob, 1)
    e_logits = (xt @ w_router_expert).astype(jnp.float32).reshape(-1, N_EXPERT_GROUPS, EXPERTS_PER_GROUP)
    e_logits = jnp.take_along_axis(e_logits, g_idx[:, :, None], axis=1)[:, 0]
    e_val, e_idx = lax.top_k(jax.nn.softmax(e_logits, axis=-1), EXPERT_TOPK)
    gate = g_val * e_val / jnp.sum(e_val, axis=-1, keepdims=True)
    expert_id = g_idx * EXPERTS_PER_GROUP + e_idx
    combine = jnp.einsum("nk,nke->ne", gate, jax.nn.one_hot(expert_id, N_EXPERTS, dtype=jnp.float32))
    y = jnp.zeros((B * T, D), jnp.float32)
    for g in range(N_EXPERT_GROUPS):
        es = slice(g * EXPERTS_PER_GROUP, (g + 1) * EXPERTS_PER_GROUP)
        hg = jax.nn.silu(jnp.einsum("nd,edf->nef", xt, w_gate[es])) * jnp.einsum("nd,edf->nef", xt, w_up[es])
        hg = hg * combine[:, es, None].astype(hg.dtype)
        y = y + jnp.einsum("nef,efd->nd", hg, w_down[es]).astype(jnp.float32)
    return y.astype(x.dtype).reshape(B, T, D)


def trunk(x, pos0, past_kv, pool_prefix, norm_mix_g, norm_ffn_g, att_w_qkv, att_q_norm_g, att_k_norm_g,
          att_w_o, pool_w, pool_scale, moe_w_router_group, moe_w_router_expert, moe_w_gate, moe_w_up, moe_w_down):
    new_k, new_v, new_pool = [], [], []
    for layer in range(DEPTH):
        h = rms_norm(x, norm_mix_g[layer])
        j = layer // 2
        if layer % 2 == 0:
            pk, pv = past_kv[j]
            y, k, v = ab_mixer(h, pk, pv, pos0, att_w_qkv[j], att_q_norm_g[j], att_k_norm_g[j], att_w_o[j])
            new_k.append(k)
            new_v.append(v)
        else:
            y, tail = pool_mixer(h, pool_prefix[j], pos0, pool_w[j], pool_scale[j])
            new_pool.append(tail)
        x = x + y
        x = x + hier_moe(rms_norm(x, norm_ffn_g[layer]), moe_w_router_group[layer], moe_w_router_expert[layer],
                         moe_w_gate[layer], moe_w_up[layer], moe_w_down[layer])
    return x, jnp.stack(new_k), jnp.stack(new_v), jnp.stack(new_pool)


def setup_inputs(seed: int = 0) -> dict:
    key = jax.random.key(seed)
    ks = jax.random.split(key, 20)
    f32 = jnp.float32
    n_pages = PAST_LEN // PAGE_SIZE
    n_used = DEC_BATCH * n_pages
    n_pool_pages = n_used + n_used // 4

    def nrm(k, shape, scale=1.0):
        return jax.random.normal(k, shape, f32) * scale

    page_table = jax.random.permutation(ks[5], n_pool_pages)[:n_used].reshape(DEC_BATCH, n_pages).astype(jnp.int32)
    return {
        "x_prompt": nrm(ks[0], (BATCH, SEQ, D_MODEL)),
        "x_sample": nrm(ks[1], (DEC_BATCH, DEC_SEQ, D_MODEL)),
        "cache_k": nrm(ks[2], (N_ATT_LAYERS, n_pool_pages, PAGE_SIZE, N_ATT_HEADS, HEAD_DIM)),
        "cache_v": nrm(ks[3], (N_ATT_LAYERS, n_pool_pages, PAGE_SIZE, N_ATT_HEADS, HEAD_DIM)),
        "state_pool": nrm(ks[4], (N_POOL_LAYERS, DEC_BATCH, POOL_STATE, D_MODEL)),
        "page_table": page_table,
        "norm_mix_g": 1.0 + nrm(ks[6], (DEPTH, D_MODEL), 0.02),
        "norm_ffn_g": 1.0 + nrm(ks[7], (DEPTH, D_MODEL), 0.02),
        "att_w_qkv": nrm(ks[8], (N_ATT_LAYERS, D_MODEL, 3 * ATT_WIDTH), D_MODEL ** -0.5),
        "att_q_norm_g": 1.0 + nrm(ks[9], (N_ATT_LAYERS, N_ATT_HEADS, HEAD_DIM), 0.02),
        "att_k_norm_g": 1.0 + nrm(ks[10], (N_ATT_LAYERS, N_ATT_HEADS, HEAD_DIM), 0.02),
        "att_w_o": nrm(ks[11], (N_ATT_LAYERS, ATT_WIDTH, D_MODEL), ATT_WIDTH ** -0.5),
        "pool_w": nrm(ks[12], (N_POOL_LAYERS, N_POOL_GROUPS, POOL_GROUP_DIM, POOL_GROUP_DIM), POOL_GROUP_DIM ** -0.5),
        "pool_scale": 1.0 + nrm(ks[13], (N_POOL_LAYERS, D_MODEL), 0.02),
        "moe_w_router_group": nrm(ks[14], (DEPTH, D_MODEL, N_EXPERT_GROUPS), D_MODEL ** -0.5),
        "moe_w_router_expert": nrm(ks[15], (DEPTH, D_MODEL, N_EXPERTS), D_MODEL ** -0.5),
        "moe_w_gate": nrm(ks[16], (DEPTH, N_EXPERTS, D_MODEL, EXPERT_FF), D_MODEL ** -0.5),
        "moe_w_up": nrm(ks[17], (DEPTH, N_EXPERTS, D_MODEL, EXPERT_FF), D_MODEL ** -0.5),
        "moe_w_down": nrm(ks[18], (DEPTH, N_EXPERTS, EXPERT_FF, D_MODEL), EXPERT_FF ** -0.5),
    }


def reference(x_prompt, x_sample, cache_k, cache_v, state_pool, page_table, norm_mix_g, norm_ffn_g,
              att_w_qkv, att_q_norm_g, att_k_norm_g, att_w_o, pool_w, pool_scale, moe_w_router_group,
              moe_w_router_expert, moe_w_gate, moe_w_up, moe_w_down):
    weights = (norm_mix_g, norm_ffn_g, att_w_qkv, att_q_norm_g, att_k_norm_g, att_w_o, pool_w, pool_scale,
               moe_w_router_group, moe_w_router_expert, moe_w_gate, moe_w_up, moe_w_down)
    n_dec = x_sample.shape[0]
    n_past = page_table.shape[1] * cache_k.shape[2]
    past_kv = [(cache_k[j, page_table].reshape(n_dec, n_past, N_ATT_HEADS, HEAD_DIM),
                cache_v[j, page_table].reshape(n_dec, n_past, N_ATT_HEADS, HEAD_DIM))
               for j in range(N_ATT_LAYERS)]
    sample_prefix = [state_pool[j] for j in range(N_POOL_LAYERS)]
    prompt_kv = [(None, None) for _ in range(N_ATT_LAYERS)]
    prompt_prefix = [jnp.zeros((x_prompt.shape[0], POOL_STATE, D_MODEL), x_prompt.dtype) for _ in range(N_POOL_LAYERS)]
    y_prompt, new_k_prompt, new_v_prompt, new_pool_prompt = trunk(x_prompt, 0, prompt_kv, prompt_prefix, *weights)
    y_sample, new_k_sample, new_v_sample, new_pool_sample = trunk(x_sample, n_past, past_kv, sample_prefix, *weights)
    return (y_prompt, y_sample, new_k_prompt, new_v_prompt, new_k_sample, new_v_sample, new_pool_prompt, new_pool_sample)
```

```python
import functools

import jax
import jax.numpy as jnp
from jax import lax
from jax.experimental import pallas as pl
from jax.experimental.pallas import tpu as pltpu

F32 = jnp.float32
BF16 = jnp.bfloat16

HEAD_DIM = 64
N_SB_HEADS = 8
N_MOBA_HEADS = 8
N_ATT_HEADS = N_SB_HEADS + N_MOBA_HEADS
ATT_WIDTH = N_ATT_HEADS * HEAD_DIM
SB_WIDTH = N_SB_HEADS * HEAD_DIM
MOBA_WIDTH = N_MOBA_HEADS * HEAD_DIM
ATT_SCALE = HEAD_DIM ** -0.5
MOBA_BLOCK = 256
MOBA_TOPK = 3
POOL_WINDOWS = (2, 4, 8, 16)
POOL_STATE = max(POOL_WINDOWS) - 1
POOL_HALO = POOL_STATE + 1
N_EXPERT_GROUPS = 4
EXPERTS_PER_GROUP = 8
N_EXPERTS = N_EXPERT_GROUPS * EXPERTS_PER_GROUP
RMS_EPS = 1e-6

LANES = 128
SUBLANES = 8
BF16_ROWS = 16
HEADS_PER_VREG = LANES // HEAD_DIM
SB_TILE = 128
NEG_BIG = -1e30
SB_LOG_CUTOFF = -104.0
VMEM_LIMIT = 48 * 1024 * 1024


def _cparams(sem):
    return pltpu.CompilerParams(dimension_semantics=sem, vmem_limit_bytes=VMEM_LIMIT)


def _dot(a, b):
    return jnp.dot(a, b, preferred_element_type=F32)


def _dot_nt(a, b):
    return lax.dot_general(a, b, (((1,), (1,)), ((), ())), preferred_element_type=F32)


def _split(a, n):
    parts = []
    for _ in range(n - 1):
        p = a.astype(BF16)
        parts.append(p)
        a = a - p.astype(F32)
    parts.append(a.astype(BF16))
    return parts


def _dot_split_lhs(a, b_exact, n):
    out = None
    for p in _split(a, n):
        t = _dot(p, b_exact)
        out = t if out is None else out + t
    return out


def _dot_split_rhs(a_exact, b, n):
    out = None
    for p in _split(b, n):
        t = _dot(a_exact, p)
        out = t if out is None else out + t
    return out


def _dot3(a, b, nt=False):
    d = _dot_nt if nt else _dot
    return d(a[0], b[0]) + (d(a[1], b[0]) + d(a[0], b[1]))


def _dot_f32(a, b, nt=False):
    a1, a2, a3 = _split(a, 3)
    b1, b2, b3 = _split(b, 3)
    d = _dot_nt if nt else _dot
    return (d(a1, b1) + (d(a1, b2) + d(a2, b1))) + (d(a1, b3) + d(a3, b1) + d(a2, b2))


def _rms(x, g):
    return x * lax.rsqrt(jnp.mean(x * x, axis=-1, keepdims=True) + RMS_EPS) * g


def _softplus(z):
    return jnp.maximum(z, 0.0) + jnp.log1p(jnp.exp(-jnp.abs(z)))


def _row_tile(n, want):
    t = min(n, want)
    assert n % t == 0, (n, t)
    return t


def _qkv_kernel(x_ref, g_ref, whi_ref, wlo_ref, qg_ref, kg_ref, seg_ref, segt_ref, q_ref, k_ref, v_ref):
    h = _rms(x_ref[...], g_ref[...])
    qkv = _dot3(_split(h, 2), (whi_ref[...], wlo_ref[...]))

    def head_norm(t, gain):
        ssq = _dot_split_lhs(t * t, seg_ref[...], 2)
        inv = lax.rsqrt(ssq * (1.0 / HEAD_DIM) + RMS_EPS)
        return t * _dot_split_lhs(inv, segt_ref[...], 2) * gain

    q = head_norm(qkv[:, :ATT_WIDTH], qg_ref[...])
    k = head_norm(qkv[:, ATT_WIDTH:2 * ATT_WIDTH], kg_ref[...])
    v = qkv[:, 2 * ATT_WIDTH:]
    q_ref[...] = q
    k_ref[...] = k
    v_ref[...] = v


def _qkv(x2d, g, w_parts, qg, kg):
    n, d = x2d.shape
    tm = _row_tile(n, 256)
    head_of_col = jnp.arange(ATT_WIDTH) // HEAD_DIM
    seg = (head_of_col[:, None] == jnp.arange(LANES)[None, :]).astype(BF16)
    segt = seg.T
    row = lambda i: (i, 0)
    fixed = lambda i: (0, 0)
    return pl.pallas_call(
        _qkv_kernel,
        grid=(n // tm,),
        in_specs=[
            pl.BlockSpec((tm, d), row),
            pl.BlockSpec((1, d), fixed),
            pl.BlockSpec((d, 3 * ATT_WIDTH), fixed),
            pl.BlockSpec((d, 3 * ATT_WIDTH), fixed),
            pl.BlockSpec((1, ATT_WIDTH), fixed),
            pl.BlockSpec((1, ATT_WIDTH), fixed),
            pl.BlockSpec((ATT_WIDTH, LANES), fixed),
            pl.BlockSpec((LANES, ATT_WIDTH), fixed),
        ],
        out_specs=[pl.BlockSpec((tm, ATT_WIDTH), row)] * 3,
        out_shape=[jax.ShapeDtypeStruct((n, ATT_WIDTH), F32)] * 3,
        compiler_params=_cparams(("parallel",)),
        name="qkv_proj",
    )(x2d, g.reshape(1, d), w_parts[0], w_parts[1], qg.reshape(1, ATT_WIDTH), kg.reshape(1, ATT_WIDTH), seg, segt)


def _sb_kernel(q_ref, k_ref, v_ref, u_ref, o_ref, acc_ref, carry_ref):
    t = SB_TILE
    i = pl.program_id(2)
    lane = lax.broadcasted_iota(jnp.int32, (t, LANES), 1)
    q2 = q_ref[...] * ATT_SCALE
    q_heads = (_split(jnp.where(lane < HEAD_DIM, q2, 0.0), 2), _split(jnp.where(lane >= HEAD_DIM, q2, 0.0), 2))
    strictly_past = (lax.broadcasted_iota(jnp.int32, (t, t), 1) < lax.broadcasted_iota(jnp.int32, (t, t), 0))
    acc_ref[...] = jnp.zeros_like(acc_ref)
    carry_ref[...] = jnp.zeros_like(carry_ref)

    def block(j, diagonal):
        start = pl.multiple_of(j * t, t)
        kb = _split(k_ref[pl.ds(start, t), :], 2)
        vb = _split(v_ref[pl.ds(start, t), :], 2)
        u = u_ref[...]
        pvs = []
        for hh in range(HEADS_PER_VREG):
            z = _dot3(q_heads[hh], kb, nt=True)
            sp = _softplus(z)
            log_keep = -sp
            if diagonal:
                log_keep = jnp.where(strictly_past, log_keep, 0.0)
            hi, lo = _split(log_keep, 2)
            carry = carry_ref[hh]
            log_between = _dot(hi, u) + _dot(lo, u) + carry
            w = jnp.exp(z - sp + log_between)
            if diagonal:
                w = jnp.where(strictly_past, w, 0.0)
            pvs.append(_dot3(_split(w, 2), vb))
            carry_ref[hh] = carry + jnp.sum(log_keep, axis=-1, keepdims=True)
        acc_ref[...] += jnp.where(lane < HEAD_DIM, pvs[0], pvs[1])

    def worst():
        return jnp.max(jnp.maximum(carry_ref[0], carry_ref[1]))

    block(i, True)

    def cond(s):
        return jnp.logical_and(s[0] >= 0, s[1] > SB_LOG_CUTOFF)

    def body(s):
        block(s[0], False)
        return s[0] - 1, worst()

    lax.while_loop(cond, body, (i - 1, worst()))
    o_ref[...] = acc_ref[...].astype(o_ref.dtype)


def _strict_upper(n):
    r = jnp.arange(n)
    return (r[:, None] > r[None, :]).astype(BF16)


def _sb_attention(q, k, v, batch, seq):
    t = SB_TILE
    assert seq % t == 0
    nq = seq // t
    n_pairs = N_SB_HEADS // HEADS_PER_VREG
    return pl.pallas_call(
        _sb_kernel,
        grid=(batch, n_pairs, nq),
        in_specs=[
            pl.BlockSpec((t, LANES), lambda b, hp, i: (b * nq + i, hp)),
            pl.BlockSpec((seq, LANES), lambda b, hp, i: (b, hp)),
            pl.BlockSpec((seq, LANES), lambda b, hp, i: (b, hp)),
            pl.BlockSpec((t, t), lambda b, hp, i: (0, 0)),
        ],
        out_specs=pl.BlockSpec((t, LANES), lambda b, hp, i: (b * nq + i, hp)),
        out_shape=jax.ShapeDtypeStruct((batch * seq, SB_WIDTH), F32),
        scratch_shapes=[pltpu.VMEM((t, LANES), F32), pltpu.VMEM((HEADS_PER_VREG, t, 1), F32)],
        compiler_params=_cparams(("parallel", "parallel", "arbitrary")),
        name="sb_attention",
    )(q, k, v, _strict_upper(t))


def _top_blocks(gate, pos, allowed, lowest_pos_wins, axis):
    g = jnp.where(allowed, gate, -jnp.inf)
    sel = jnp.zeros(gate.shape, F32)
    for _ in range(MOBA_TOPK):
        m = jnp.max(g, axis=axis, keepdims=True)
        hit = g == m
        if lowest_pos_wins:
            idx = jnp.min(jnp.where(hit, pos, jnp.int32(2 ** 30)), axis=axis, keepdims=True)
        else:
            idx = jnp.max(jnp.where(hit, pos, jnp.int32(-1)), axis=axis, keepdims=True)
        at = pos == idx
        sel = jnp.where(jnp.logical_and(at, m > -jnp.inf), 1.0, sel)
        g = jnp.where(at, -jnp.inf, g)
    return sel


def _moba_kernel(q_ref, k_ref, v_ref, o_ref, kmean_ref, *, n_blocks):
    t = MOBA_BLOCK
    i = pl.program_id(2)

    @pl.when(i == 0)
    def _():
        kmean_ref[...] = jnp.zeros_like(kmean_ref)
        for n in range(n_blocks):
            kmean_ref[n:n + 1, :] = jnp.mean(k_ref[n * t:(n + 1) * t, :], axis=0, keepdims=True)

    lane = lax.broadcasted_iota(jnp.int32, (t, LANES), 1)
    q = q_ref[...]
    first = lane < HEAD_DIM
    q_f32 = (jnp.where(first, q, 0.0), jnp.where(first, 0.0, q))
    q_heads = tuple(_split(qh * ATT_SCALE, 2) for qh in q_f32)
    kmean = kmean_ref[...]
    sels = [_top_blocks(_dot_f32(q_f32[hh], kmean, nt=True), lane, lane < i, True, 1).astype(BF16)
            for hh in range(HEADS_PER_VREG)]

    own = pl.multiple_of(i * t, t)
    kb = _split(k_ref[pl.ds(own, t), :], 2)
    vb = _split(v_ref[pl.ds(own, t), :], 2)
    causal = (lax.broadcasted_iota(jnp.int32, (t, t), 1) <= lax.broadcasted_iota(jnp.int32, (t, t), 0))
    state = []
    for hh in range(HEADS_PER_VREG):
        s = jnp.where(causal, _dot3(q_heads[hh], kb, nt=True), NEG_BIG)
        m = jnp.max(s, axis=-1, keepdims=True)
        p = jnp.exp(s - m)
        state += [m, jnp.sum(p, axis=-1, keepdims=True), _dot3(_split(p, 2), vb)]

    def body(j, state):
        start = pl.multiple_of(j * t, t)
        kb = _split(k_ref[pl.ds(start, t), :], 2)
        vb = _split(v_ref[pl.ds(start, t), :], 2)
        pick_col = (lax.broadcasted_iota(jnp.int32, (LANES, t), 0) == j).astype(BF16)
        new = []
        for hh in range(HEADS_PER_VREG):
            m, l, acc = state[3 * hh:3 * hh + 3]
            chosen = _dot(sels[hh], pick_col) > 0.5
            s = jnp.where(chosen, _dot3(q_heads[hh], kb, nt=True), NEG_BIG)
            m_new = jnp.maximum(m, jnp.max(s, axis=-1, keepdims=True))
            alpha = jnp.exp(m - m_new)
            p = jnp.exp(s - m_new)
            new += [m_new, alpha * l + jnp.sum(p, axis=-1, keepdims=True), alpha * acc + _dot3(_split(p, 2), vb)]
        return tuple(new)

    state = lax.fori_loop(0, i, body, tuple(state))
    outs = [state[3 * hh + 2] / state[3 * hh + 1] for hh in range(HEADS_PER_VREG)]
    o_ref[...] = jnp.where(first, outs[0], outs[1]).astype(o_ref.dtype)


def _moba_attention(q, k, v, batch, seq):
    t = MOBA_BLOCK
    assert seq % t == 0 and seq // t <= LANES
    nq = seq // t
    n_pairs = N_MOBA_HEADS // HEADS_PER_VREG
    first_pair = N_SB_HEADS // HEADS_PER_VREG
    q_map = lambda b, hp, i: (b * nq + i, first_pair + hp)
    kv_map = lambda b, hp, i: (b, first_pair + hp)
    return pl.pallas_call(
        functools.partial(_moba_kernel, n_blocks=nq),
        grid=(batch, n_pairs, nq),
        in_specs=[
            pl.BlockSpec((t, LANES), q_map),
            pl.BlockSpec((seq, LANES), kv_map),
            pl.BlockSpec((seq, LANES), kv_map),
        ],
        out_specs=pl.BlockSpec((t, LANES), lambda b, hp, i: (b * nq + i, hp)),
        out_shape=jax.ShapeDtypeStruct((batch * seq, MOBA_WIDTH), F32),
        scratch_shapes=[pltpu.VMEM((LANES, LANES), F32)],
        compiler_params=_cparams(("parallel", "parallel", "arbitrary")),
        name="moba_attention",
    )(q, k, v)


def _put_row(ref, p, row):
    base = pl.multiple_of((p // SUBLANES) * SUBLANES, SUBLANES)
    cur = ref[pl.ds(base, SUBLANES), :]
    sub = lax.broadcasted_iota(jnp.int32, cur.shape, 0)
    ref[pl.ds(base, SUBLANES), :] = jnp.where(sub == p % SUBLANES, row, cur)


def _put_col(ref, p, col):
    cur = ref[...]
    lane = lax.broadcasted_iota(jnp.int32, cur.shape, 1)
    ref[...] = jnp.where(lane == p, col, cur)


def _sample_attn_kernel(pt_ref, qrow_ref, knew_ref, vnew_ref, k0_ref, k1_ref, v0_ref, v1_ref, u_ref,
                        o_ref, carry_ref, osb_ref, m_ref, l_ref, ks_ref, omb_ref, *, n_blocks, n_new):
    del pt_ref
    p = pl.program_id(1)
    group_rows = n_new * N_SB_HEADS
    n_rows = 2 * group_rows

    @pl.when(p == 0)
    def _():
        carry_ref[...] = jnp.zeros_like(carry_ref)
        osb_ref[...] = jnp.zeros_like(osb_ref)
        m_ref[...] = jnp.full(m_ref.shape, NEG_BIG, F32)
        l_ref[...] = jnp.zeros_like(l_ref)
        ks_ref[...] = jnp.zeros_like(ks_ref)
        omb_ref[...] = jnp.zeros_like(omb_ref)

    head_diag = (lax.broadcasted_iota(jnp.int32, (N_MOBA_HEADS, MOBA_WIDTH), 1) // HEAD_DIM
                 == lax.broadcasted_iota(jnp.int32, (N_MOBA_HEADS, MOBA_WIDTH), 0)).astype(F32)

    def process(kt, vt, fresh):
        keys = kt.shape[1]
        z = _dot3(_split(qrow_ref[0] * ATT_SCALE, 2), _split(kt, 2))
        row = lax.broadcasted_iota(jnp.int32, (n_rows, keys), 0)
        sp = _softplus(z)
        log_keep = -sp
        s_mb = z
        if fresh:
            key = lax.broadcasted_iota(jnp.int32, (n_rows, keys), 1)
            tok = (row % group_rows) // N_SB_HEADS
            ok_sb = key < tok
            log_keep = jnp.where(ok_sb, log_keep, 0.0)
            s_mb = jnp.where(key <= tok, z, NEG_BIG)
        u = u_ref[0:keys, 0:keys]
        carry = carry_ref[...]
        log_between = _dot_split_lhs(log_keep, u, 2) + carry
        w_sb = jnp.exp(z - sp + log_between)
        if fresh:
            w_sb = jnp.where(ok_sb, w_sb, 0.0)
        carry_ref[...] = carry + jnp.sum(log_keep, axis=1, keepdims=True)
        m_blk = jnp.max(s_mb, axis=1, keepdims=True)
        e = jnp.exp(s_mb - m_blk)
        _put_col(m_ref, p, m_blk)
        _put_col(l_ref, p, jnp.sum(e, axis=1, keepdims=True))
        w = _split(jnp.where(row < group_rows, w_sb, e), 2)
        vb = _split(vt, 2)
        half = lambda parts, lo, hi: tuple(part[lo:hi] for part in parts)
        osb_ref[...] += _dot3(half(w, 0, group_rows), half(vb, 0, SB_WIDTH), nt=True)
        o_blk = _dot3(half(w, group_rows, n_rows), half(vb, SB_WIDTH, ATT_WIDTH), nt=True)
        for tkn in range(n_new):
            part = o_blk[tkn * N_MOBA_HEADS:(tkn + 1) * N_MOBA_HEADS] * head_diag
            _put_row(omb_ref.at[tkn], p, jnp.sum(part, axis=0, keepdims=True))
        _put_col(ks_ref, p, jnp.sum(kt[SB_WIDTH:], axis=1, keepdims=True))

    @pl.when(p == 0)
    def _():
        process(knew_ref[0], vnew_ref[0], True)

    @pl.when(p > 0)
    def _():
        process(jnp.concatenate([k0_ref[0], k1_ref[0]], axis=1),
                jnp.concatenate([v0_ref[0], v1_ref[0]], axis=1), False)

    @pl.when(p == n_blocks)
    def _():
        lane = lax.broadcasted_iota(jnp.int32, (n_rows, LANES), 1)
        gate = _dot_f32(qrow_ref[0][:, SB_WIDTH:], ks_ref[...])
        past = jnp.logical_and(lane >= 1, lane <= n_blocks)
        sel = _top_blocks(gate, lane, past, False, 1)
        sel = jnp.logical_or(sel > 0.5, lane == 0)
        m_all = m_ref[...]
        m_top = jnp.max(jnp.where(sel, m_all, NEG_BIG), axis=1, keepdims=True)
        coef = jnp.where(sel, jnp.exp(m_all - m_top), 0.0)
        coef = coef / jnp.sum(coef * l_ref[...], axis=1, keepdims=True)
        coef_t = jnp.concatenate([coef, jnp.zeros((LANES - n_rows, LANES), F32)], axis=0).T
        coef_t = coef_t[0:omb_ref.shape[1]]
        col_head = lax.broadcasted_iota(jnp.int32, (LANES, MOBA_WIDTH), 1) // HEAD_DIM
        src_row = lax.broadcasted_iota(jnp.int32, (LANES, MOBA_WIDTH), 0)
        osb = osb_ref[...]
        for tkn in range(n_new):
            spread = (src_row == group_rows + tkn * N_MOBA_HEADS + col_head).astype(BF16)
            weighted = _dot_split_lhs(coef_t, spread, 3) * omb_ref[tkn]
            o_ref[0, tkn:tkn + 1, SB_WIDTH:] = jnp.sum(weighted, axis=0, keepdims=True)
            part = osb[tkn * N_SB_HEADS:(tkn + 1) * N_SB_HEADS] * head_diag
            o_ref[0, tkn:tkn + 1, 0:SB_WIDTH] = jnp.sum(part, axis=0, keepdims=True)


def _sample_attention(q, k_new, v_new, cache_kt, cache_vt, page_table):
    n_seq, n_new, _ = q.shape
    _, _, page = cache_kt.shape
    n_pages = page_table.shape[1]
    assert MOBA_BLOCK == 2 * page and page == LANES and n_pages % 2 == 0 and n_new <= SUBLANES
    assert N_SB_HEADS == N_MOBA_HEADS and 2 * n_new * N_SB_HEADS <= LANES
    n_blocks = n_pages // 2
    steps = n_blocks + 1
    assert steps <= LANES
    steps_pad = -(-steps // BF16_ROWS) * BF16_ROWS
    n_rows = 2 * n_new * N_SB_HEADS

    qh = q.reshape(n_seq, n_new, 2, N_SB_HEADS, HEAD_DIM)
    eye_g = jnp.eye(2, dtype=F32)
    eye_h = jnp.eye(N_SB_HEADS, dtype=F32)
    qrow = jnp.einsum("btghd,gG,hH->bGtHghd", qh, eye_g, eye_h).reshape(n_seq, n_rows, ATT_WIDTH)
    new_t = lambda a: jnp.pad(jnp.swapaxes(a, 1, 2), ((0, 0), (0, 0), (0, page - n_new)))

    def page_map(which):
        def index(b, p, pt):
            blk = n_blocks - jnp.maximum(p, 1)
            return (pt[b * n_pages + 2 * blk + which], 0, 0)
        return index

    seq_map = lambda b, p, pt: (b, 0, 0)
    page_spec = lambda which: pl.BlockSpec((1, ATT_WIDTH, page), page_map(which))
    grid_spec = pltpu.PrefetchScalarGridSpec(
        num_scalar_prefetch=1,
        grid=(n_seq, steps),
        in_specs=[
            pl.BlockSpec((1, n_rows, ATT_WIDTH), seq_map),
            pl.BlockSpec((1, ATT_WIDTH, page), seq_map),
            pl.BlockSpec((1, ATT_WIDTH, page), seq_map),
            page_spec(0), page_spec(1), page_spec(0), page_spec(1),
            pl.BlockSpec((MOBA_BLOCK, MOBA_BLOCK), lambda b, p, pt: (0, 0)),
        ],
        out_specs=pl.BlockSpec((1, n_new, ATT_WIDTH), seq_map),
        scratch_shapes=[
            pltpu.VMEM((n_rows, 1), F32),
            pltpu.VMEM((n_new * N_SB_HEADS, SB_WIDTH), F32),
            pltpu.VMEM((n_rows, LANES), F32),
            pltpu.VMEM((n_rows, LANES), F32),
            pltpu.VMEM((MOBA_WIDTH, LANES), F32),
            pltpu.VMEM((n_new, steps_pad, MOBA_WIDTH), F32),
        ],
    )
    return pl.pallas_call(
        functools.partial(_sample_attn_kernel, n_blocks=n_blocks, n_new=n_new),
        grid_spec=grid_spec,
        out_shape=jax.ShapeDtypeStruct((n_seq, n_new, ATT_WIDTH), F32),
        compiler_params=_cparams(("parallel", "arbitrary")),
        name="sample_attention",
    )(page_table.reshape(-1), qrow, new_t(k_new), new_t(v_new),
      cache_kt, cache_kt, cache_vt, cache_vt, _strict_upper(MOBA_BLOCK))


def _attn_out_kernel(x_ref, osb_ref, omb_ref, wsb_hi_ref, wsb_lo_ref, wmb_hi_ref, wmb_lo_ref, y_ref):
    sb = _dot3(_split(osb_ref[...], 2), (wsb_hi_ref[...], wsb_lo_ref[...]))
    mb = _dot3(_split(omb_ref[...], 2), (wmb_hi_ref[...], wmb_lo_ref[...]))
    y_ref[...] = x_ref[...] + (sb + mb)


def _attn_out(x2d, o_sb, o_mb, w_o_parts):
    n, d = x2d.shape
    tm = _row_tile(n, 512)
    row = lambda i: (i, 0)
    fixed = lambda i: (0, 0)
    return pl.pallas_call(
        _attn_out_kernel,
        grid=(n // tm,),
        in_specs=[
            pl.BlockSpec((tm, d), row),
            pl.BlockSpec((tm, SB_WIDTH), row),
            pl.BlockSpec((tm, MOBA_WIDTH), row),
            pl.BlockSpec((SB_WIDTH, d), fixed),
            pl.BlockSpec((SB_WIDTH, d), fixed),
            pl.BlockSpec((MOBA_WIDTH, d), lambda i: (1, 0)),
            pl.BlockSpec((MOBA_WIDTH, d), lambda i: (1, 0)),
        ],
        out_specs=pl.BlockSpec((tm, d), row),
        out_shape=jax.ShapeDtypeStruct((n, d), F32),
        compiler_params=_cparams(("parallel",)),
        name="attn_out_proj",
    )(x2d, o_sb, o_mb, w_o_parts[0], w_o_parts[1], w_o_parts[0], w_o_parts[1])


def _pool_kernel(*refs, pos0, rows, has_prev):
    if has_prev:
        x_ref, xprev_ref, pre_ref, g_ref, w_ref, sc_ref, y_ref, tail_ref = refs
    else:
        x_ref, pre_ref, g_ref, w_ref, sc_ref, y_ref, tail_ref = refs
    ti = pl.program_id(1)
    g = g_ref[...]
    x = x_ref[0]
    d = x.shape[-1]
    h = _rms(x, g)
    halo = pre_ref[0]
    if has_prev:
        halo = jnp.where(ti == 0, halo, _rms(xprev_ref[0], g))
    pad = (-rows) % SUBLANES
    pieces = [halo, h] + ([jnp.zeros((pad, d), F32)] if pad else [])
    full = jnp.concatenate(pieces, axis=0)
    tail_ref[0] = full[rows:rows + POOL_HALO]

    group = d // len(POOL_WINDOWS)
    pos = pos0 + ti * rows + lax.broadcasted_iota(jnp.int32, (rows, 1), 0)
    s = full
    ys = []
    for gi, win in enumerate(POOL_WINDOWS):
        s = s[:, (group if gi else 0):]
        s = s + pltpu.roll(s, win // 2, 0)
        count = jnp.minimum(win, pos + 1).astype(F32)
        diff = s[POOL_HALO:POOL_HALO + rows, :group] / count - h[:, gi * group:(gi + 1) * group]
        ys.append(_dot(diff.astype(BF16), w_ref[gi]))
    y_ref[0] = x + jnp.concatenate(ys, axis=-1) * sc_ref[...]


def _pool_mixer(x, prefix_h, g, w_pool_bf16, scale, pos0):
    b, t, d = x.shape
    rows = min(t, 512)
    assert t % rows == 0 and rows % POOL_HALO == 0 or t == rows
    has_prev = t > rows
    nt = t // rows
    per = rows // POOL_HALO if has_prev else 1
    in_specs = [pl.BlockSpec((1, rows, d), lambda bi, ti: (bi, ti, 0))]
    args = [x]
    if has_prev:
        in_specs.append(pl.BlockSpec((1, POOL_HALO, d), lambda bi, ti: (bi, jnp.maximum(ti * per - 1, 0), 0)))
        args.append(x)
    fixed2 = lambda bi, ti: (0, 0)
    in_specs += [
        pl.BlockSpec((1, POOL_HALO, d), lambda bi, ti: (bi, 0, 0)),
        pl.BlockSpec((1, d), fixed2),
        pl.BlockSpec(w_pool_bf16.shape, lambda bi, ti: (0, 0, 0)),
        pl.BlockSpec((1, d), fixed2),
    ]
    args += [prefix_h, g.reshape(1, d), w_pool_bf16, scale.reshape(1, d)]
    return pl.pallas_call(
        functools.partial(_pool_kernel, pos0=pos0, rows=rows, has_prev=has_prev),
        grid=(b, nt),
        in_specs=in_specs,
        out_specs=[
            pl.BlockSpec((1, rows, d), lambda bi, ti: (bi, ti, 0)),
            pl.BlockSpec((1, POOL_HALO, d), lambda bi, ti: (bi, 0, 0)),
        ],
        out_shape=[jax.ShapeDtypeStruct((b, t, d), F32), jax.ShapeDtypeStruct((b, POOL_HALO, d), F32)],
        compiler_params=_cparams(("parallel", "arbitrary")),
        name="pool_mixer",
    )(*args)


def _router_kernel(x_ref, g_ref, wr_ref, xn_ref, comb_ref):
    xn = _rms(x_ref[...], g_ref[...])
    xn_ref[...] = xn.astype(BF16)
    logits = _dot_f32(xn, wr_ref[...])
    lane = lax.broadcasted_iota(jnp.int32, logits.shape, 1)
    is_group = jnp.logical_and(lane >= N_EXPERTS, lane < N_EXPERTS + N_EXPERT_GROUPS)
    glog = jnp.where(is_group, logits, -jnp.inf)
    gmax = jnp.max(glog, axis=-1, keepdims=True)
    gsum = jnp.sum(jnp.exp(glog - gmax), axis=-1, keepdims=True)
    g_val = 1.0 / gsum
    g_idx = jnp.min(jnp.where(glog == gmax, lane, 2 * LANES), axis=-1, keepdims=True) - N_EXPERTS
    in_group = jnp.logical_and(lane >= g_idx * EXPERTS_PER_GROUP, lane < (g_idx + 1) * EXPERTS_PER_GROUP)
    elog = jnp.where(in_group, logits, -jnp.inf)
    eexp = jnp.exp(elog - jnp.max(elog, axis=-1, keepdims=True))
    prob = eexp / jnp.sum(eexp, axis=-1, keepdims=True)
    p1 = jnp.max(prob, axis=-1, keepdims=True)
    i1 = jnp.min(jnp.where(jnp.logical_and(in_group, prob == p1), lane, 2 * LANES), axis=-1, keepdims=True)
    rest = jnp.where(jnp.logical_and(in_group, lane != i1), prob, -1.0)
    p2 = jnp.max(rest, axis=-1, keepdims=True)
    i2 = jnp.min(jnp.where(rest == p2, lane, 2 * LANES), axis=-1, keepdims=True)
    norm = g_val / (p1 + p2)
    comb_ref[...] = jnp.where(lane == i1, p1 * norm, 0.0) + jnp.where(lane == i2, p2 * norm, 0.0)


def _router(x2d, g, w_router):
    n, d = x2d.shape
    tm = _row_tile(n, 512)
    row = lambda i: (i, 0)
    fixed = lambda i: (0, 0)
    return pl.pallas_call(
        _router_kernel,
        grid=(n // tm,),
        in_specs=[pl.BlockSpec((tm, d), row), pl.BlockSpec((1, d), fixed), pl.BlockSpec((d, LANES), fixed)],
        out_specs=[pl.BlockSpec((tm, d), row), pl.BlockSpec((tm, LANES), row)],
        out_shape=[jax.ShapeDtypeStruct((n, d), BF16), jax.ShapeDtypeStruct((n, LANES), F32)],
        compiler_params=_cparams(("parallel",)),
        name="ffn_router",
    )(x2d, g.reshape(1, d), w_router)


def _moe_kernel(x_ref, xn_ref, comb_ref, wg_ref, wu_ref, wd_ref, y_ref):
    e = pl.program_id(1)

    @pl.when(e == 0)
    def _():
        y_ref[...] = x_ref[...]

    xn = xn_ref[...]
    gate = _dot(xn, wg_ref[0])
    up = _dot(xn, wu_ref[0])
    ff = gate.shape[-1]
    pick = (lax.broadcasted_iota(jnp.int32, (LANES, ff), 0) == e).astype(BF16)
    weight = _dot_split_lhs(comb_ref[...], pick, 3)
    hidden = gate * jax.nn.sigmoid(gate) * up * weight
    y_ref[...] += _dot(hidden.astype(BF16), wd_ref[0])


def _moe(x2d, xn, comb, wg, wu, wd):
    n, d = x2d.shape
    n_exp, _, ff = wg.shape
    tm = _row_tile(n, 1024)
    row = lambda i, e: (i, 0)
    return pl.pallas_call(
        _moe_kernel,
        grid=(n // tm, n_exp),
        in_specs=[
            pl.BlockSpec((tm, d), row),
            pl.BlockSpec((tm, d), row),
            pl.BlockSpec((tm, LANES), row),
            pl.BlockSpec((1, d, ff), lambda i, e: (e, 0, 0)),
            pl.BlockSpec((1, d, ff), lambda i, e: (e, 0, 0)),
            pl.BlockSpec((1, ff, d), lambda i, e: (e, 0, 0)),
        ],
        out_specs=pl.BlockSpec((tm, d), row),
        out_shape=jax.ShapeDtypeStruct((n, d), F32),
        compiler_params=_cparams(("parallel", "arbitrary")),
        name="moe_experts",
    )(x2d, xn, comb, wg, wu, wd)


def _router_weight(w_group, w_expert):
    d = w_group.shape[0]
    w = jnp.concatenate([w_expert, w_group], axis=1)
    return jnp.pad(w, ((0, 0), (0, LANES - w.shape[1]))).astype(F32).reshape(d, LANES)


def _cache_pages_t(cache):
    layers, pages, page, heads, dim = cache.shape
    return jnp.transpose(cache, (0, 1, 3, 4, 2)).reshape(layers * pages, heads * dim, page)


def _ffn(x2d, g, w_router, wg, wu, wd):
    xn, comb = _router(x2d, g, w_router)
    return _moe(x2d, xn, comb, wg, wu, wd)


def kernel(x_prompt, x_sample, cache_k, cache_v, state_pool, page_table, norm_mix_g, norm_ffn_g, att_w_qkv, att_q_norm_g, att_k_norm_g, att_w_o, pool_w, pool_scale, moe_w_router_group, moe_w_router_expert, moe_w_gate, moe_w_up, moe_w_down):
    batch, seq, d = x_prompt.shape
    n_seq, n_new, _ = x_sample.shape
    depth = norm_mix_g.shape[0]
    page = cache_k.shape[2]
    n_past = page_table.shape[1] * page
    assert depth == 2 and d == ATT_WIDTH and n_past % MOBA_BLOCK == 0

    w_qkv = _split(att_w_qkv[0], 2)
    w_o = _split(att_w_o[0], 2)
    w_pool = pool_w.astype(BF16)
    wg, wu, wd = moe_w_gate.astype(BF16), moe_w_up.astype(BF16), moe_w_down.astype(BF16)
    w_router = [_router_weight(moe_w_router_group[l], moe_w_router_expert[l]) for l in range(depth)]
    ffn = lambda x2d, l: _ffn(x2d, norm_ffn_g[l], w_router[l], wg[l], wu[l], wd[l])

    xp = x_prompt.reshape(batch * seq, d)
    xs = x_sample.reshape(n_seq * n_new, d)

    qg, kg = att_q_norm_g[0].reshape(-1), att_k_norm_g[0].reshape(-1)
    q_p, k_p, v_p = _qkv(xp, norm_mix_g[0], w_qkv, qg, kg)
    o_sb = _sb_attention(q_p, k_p, v_p, batch, seq)
    o_mb = _moba_attention(q_p, k_p, v_p, batch, seq)
    xp = _attn_out(xp, o_sb, o_mb, w_o)

    q_s, k_s, v_s = _qkv(xs, norm_mix_g[0], w_qkv, qg, kg)
    o_s = _sample_attention(
        q_s.reshape(n_seq, n_new, d), k_s.reshape(n_seq, n_new, d), v_s.reshape(n_seq, n_new, d),
        _cache_pages_t(cache_k), _cache_pages_t(cache_v), page_table)
    o_s = o_s.reshape(n_seq * n_new, d)
    xs = _attn_out(xs, o_s[:, :SB_WIDTH], o_s[:, SB_WIDTH:], w_o)

    xp = ffn(xp, 0)
    xs = ffn(xs, 0)

    zeros_p = jnp.zeros((batch, POOL_HALO, d), F32)
    xp3, tail_p = _pool_mixer(xp.reshape(batch, seq, d), zeros_p, norm_mix_g[1], w_pool[0], pool_scale[0], 0)
    prefix_s = jnp.pad(state_pool[0], ((0, 0), (POOL_HALO - POOL_STATE, 0), (0, 0)))
    xs3, tail_s = _pool_mixer(xs.reshape(n_seq, n_new, d), prefix_s, norm_mix_g[1], w_pool[0], pool_scale[0], n_past)
    xp = ffn(xp3.reshape(batch * seq, d), 1)
    xs = ffn(xs3.reshape(n_seq * n_new, d), 1)

    heads = (N_ATT_HEADS, HEAD_DIM)
    return (
        xp.reshape(batch, seq, d),
        xs.reshape(n_seq, n_new, d),
        k_p.reshape(1, batch, seq, *heads),
        v_p.reshape(1, batch, seq, *heads),
        k_s.reshape(1, n_seq, n_new, *heads),
        v_s.reshape(1, n_seq, n_new, *heads),
        tail_p[None, :, POOL_HALO - POOL_STATE:],
        tail_s[None, :, POOL_HALO - POOL_STATE:],
    )
```

```python
import functools

import jax
import jax.numpy as jnp
from jax import lax
from jax.experimental import pallas as pl
from jax.experimental.pallas import tpu as pltpu

F32 = jnp.float32
BF16 = jnp.bfloat16

HEAD_DIM = 64
N_SB_HEADS = 8
N_MOBA_HEADS = 8
N_ATT_HEADS = N_SB_HEADS + N_MOBA_HEADS
ATT_WIDTH = N_ATT_HEADS * HEAD_DIM
SB_WIDTH = N_SB_HEADS * HEAD_DIM
MOBA_WIDTH = N_MOBA_HEADS * HEAD_DIM
ATT_SCALE = HEAD_DIM ** -0.5
MOBA_BLOCK = 256
MOBA_TOPK = 3
MAX_MOBA_BLOCKS = 24
POOL_WINDOWS = (2, 4, 8, 16)
POOL_STATE = max(POOL_WINDOWS) - 1
POOL_HALO = POOL_STATE + 1
N_EXPERT_GROUPS = 4
EXPERTS_PER_GROUP = 8
N_EXPERTS = N_EXPERT_GROUPS * EXPERTS_PER_GROUP
RMS_EPS = 1e-6

LANES = 128
SUBLANES = 8
BF16_ROWS = 16
HEADS_PER_VREG = LANES // HEAD_DIM
SB_TILE = 128
NEG_BIG = -1e30
SB_LOG_CUTOFF = -104.0
VMEM_LIMIT = 48 * 1024 * 1024


def _cparams(sem):
    return pltpu.CompilerParams(dimension_semantics=sem, vmem_limit_bytes=VMEM_LIMIT)


def _dot(a, b):
    return jnp.dot(a, b, preferred_element_type=F32)


def _dot_nt(a, b):
    return lax.dot_general(a, b, (((1,), (1,)), ((), ())), preferred_element_type=F32)


def _split(a, n):
    parts = []
    for _ in range(n - 1):
        p = a.astype(BF16)
        parts.append(p)
        a = a - p.astype(F32)
    parts.append(a.astype(BF16))
    return parts


def _split_weights(w):
    hi = lax.bitcast_convert_type(lax.bitcast_convert_type(w, jnp.uint32) & jnp.uint32(0xFFFF0000), F32)
    return hi.astype(BF16), (w - hi).astype(BF16)


def _dot_split_lhs(a, b_exact, n):
    out = None
    for p in _split(a, n):
        t = _dot(p, b_exact)
        out = t if out is None else out + t
    return out


def _dot_split_rhs(a_exact, b, n):
    out = None
    for p in _split(b, n):
        t = _dot(a_exact, p)
        out = t if out is None else out + t
    return out


def _dot3(a, b, nt=False):
    d = _dot_nt if nt else _dot
    return d(a[0], b[0]) + (d(a[1], b[0]) + d(a[0], b[1]))


def _dot_f32(a, b, nt=False):
    a1, a2, a3 = _split(a, 3)
    b1, b2, b3 = _split(b, 3)
    d = _dot_nt if nt else _dot
    return (d(a1, b1) + (d(a1, b2) + d(a2, b1))) + (d(a1, b3) + d(a3, b1) + d(a2, b2))


def _rms(x, g):
    return x * lax.rsqrt(jnp.mean(x * x, axis=-1, keepdims=True) + RMS_EPS) * g


def _softplus(z):
    return jnp.maximum(z, 0.0) + jnp.log1p(jnp.exp(-jnp.abs(z)))


def _row_tile(n, want):
    t = min(n, want)
    assert n % t == 0, (n, t)
    return t


def _qkv_kernel(x_ref, g_ref, whi_ref, wlo_ref, qg_ref, kg_ref, seg_ref, segt_ref, q_ref, k_ref, v_ref):
    h = _rms(x_ref[...], g_ref[...])
    qkv = _dot3(_split(h, 2), (whi_ref[...], wlo_ref[...]))

    def head_norm(t, gain):
        ssq = _dot_split_lhs(t * t, seg_ref[...], 2)
        inv = lax.rsqrt(ssq * (1.0 / HEAD_DIM) + RMS_EPS)
        return t * _dot_split_lhs(inv, segt_ref[...], 2) * gain

    q = head_norm(qkv[:, :ATT_WIDTH], qg_ref[...])
    k = head_norm(qkv[:, ATT_WIDTH:2 * ATT_WIDTH], kg_ref[...])
    v = qkv[:, 2 * ATT_WIDTH:]
    q_ref[...] = q
    k_ref[...] = k
    v_ref[...] = v


def _qkv(x2d, g, w_parts, qg, kg):
    n, d = x2d.shape
    tm = _row_tile(n, 256)
    head_of_col = jnp.arange(ATT_WIDTH) // HEAD_DIM
    seg = (head_of_col[:, None] == jnp.arange(LANES)[None, :]).astype(BF16)
    segt = seg.T
    row = lambda i: (i, 0)
    fixed = lambda i: (0, 0)
    return pl.pallas_call(
        _qkv_kernel,
        grid=(n // tm,),
        in_specs=[
            pl.BlockSpec((tm, d), row),
            pl.BlockSpec((1, d), fixed),
            pl.BlockSpec((d, 3 * ATT_WIDTH), fixed),
            pl.BlockSpec((d, 3 * ATT_WIDTH), fixed),
            pl.BlockSpec((1, ATT_WIDTH), fixed),
            pl.BlockSpec((1, ATT_WIDTH), fixed),
            pl.BlockSpec((ATT_WIDTH, LANES), fixed),
            pl.BlockSpec((LANES, ATT_WIDTH), fixed),
        ],
        out_specs=[pl.BlockSpec((tm, ATT_WIDTH), row)] * 3,
        out_shape=[jax.ShapeDtypeStruct((n, ATT_WIDTH), F32)] * 3,
        compiler_params=_cparams(("parallel",)),
        name="qkv_proj",
    )(x2d, g.reshape(1, d), w_parts[0], w_parts[1], qg.reshape(1, ATT_WIDTH), kg.reshape(1, ATT_WIDTH), seg, segt)


def _hi_lo(x):
    hi = x.astype(BF16).astype(F32)
    return hi, x - hi


def _swap_heads(x):
    return pltpu.roll(x, HEAD_DIM, 1)


def _stack_queries(q, first):
    hi, lo = _hi_lo(q)
    hs, ls = _swap_heads(hi), _swap_heads(lo)
    head_a = jnp.concatenate([jnp.where(first, hi, ls), jnp.where(first, hi, 0.0)], axis=1)
    head_b = jnp.concatenate([jnp.where(first, hs, lo), jnp.where(first, hs, 0.0)], axis=1)
    return head_a.astype(BF16), head_b.astype(BF16)


def _stack_keys(k, first):
    hi, lo = _hi_lo(k)
    hs, ls = _swap_heads(hi), _swap_heads(lo)
    head_a = jnp.concatenate([jnp.where(first, hi, hs), jnp.where(first, lo, 0.0)], axis=1)
    head_b = jnp.concatenate([jnp.where(first, hs, hi), jnp.where(first, ls, 0.0)], axis=1)
    return head_a.astype(BF16), head_b.astype(BF16)


def _stack_values(v, first):
    hi, lo = _hi_lo(v)
    hs, ls = _swap_heads(hi), _swap_heads(lo)
    return ((jnp.where(first, hi, ls), jnp.where(first, hi, 0.0)),
            (jnp.where(first, hs, lo), jnp.where(first, hs, 0.0)))


def _fold_heads(acc_a, acc_b, first):
    return jnp.where(first, acc_a + _swap_heads(acc_a), acc_b + _swap_heads(acc_b))


def _sb_kernel(q_ref, k_ref, v_ref, u2_ref, o_ref, kcat_ref, vcat_ref, acc_ref, carry_ref):
    t = SB_TILE
    i = pl.program_id(2)
    first = lax.broadcasted_iota(jnp.int32, (t, LANES), 1) < HEAD_DIM

    @pl.when(i == 0)
    def _():
        def build(j, c):
            rows = pl.ds(pl.multiple_of(j * t, t), t)
            for hh, kc in enumerate(_stack_keys(k_ref[rows, :], first)):
                kcat_ref[hh, rows, :] = kc
            for hh, (top, bottom) in enumerate(_stack_values(v_ref[rows, :], first)):
                vcat_ref[hh, j] = jnp.concatenate([top, bottom], axis=0).astype(BF16)
            return c
        lax.fori_loop(0, k_ref.shape[0] // t, build, 0)

    q_heads = _stack_queries(q_ref[...] * ATT_SCALE, first)
    strictly_past = (lax.broadcasted_iota(jnp.int32, (t, t), 1) < lax.broadcasted_iota(jnp.int32, (t, t), 0))
    acc_ref[...] = jnp.zeros_like(acc_ref)
    carry_ref[...] = jnp.zeros_like(carry_ref)

    def block(j, diagonal):
        rows = pl.ds(pl.multiple_of(j * t, t), t)
        for hh in range(HEADS_PER_VREG):
            z = _dot_nt(q_heads[hh], kcat_ref[hh, rows, :])
            sp = _softplus(z)
            log_keep = -sp
            if diagonal:
                log_keep = jnp.where(strictly_past, log_keep, 0.0)
            carry = carry_ref[hh]
            log_between = _dot(jnp.concatenate(_split(log_keep, 2), axis=1), u2_ref[...]) + carry
            w = jnp.exp(z - sp + log_between)
            if diagonal:
                w = jnp.where(strictly_past, w, 0.0)
            acc_ref[hh] += _dot(jnp.concatenate(_split(w, 2), axis=1), vcat_ref[hh, j])
            carry_ref[hh] = carry + jnp.sum(log_keep, axis=-1, keepdims=True)

    def worst():
        return jnp.max(jnp.maximum(carry_ref[0], carry_ref[1]))

    block(i, True)

    def cond(s):
        return jnp.logical_and(s[0] >= 0, s[1] > SB_LOG_CUTOFF)

    def body(s):
        block(s[0], False)
        return s[0] - 1, worst()

    lax.while_loop(cond, body, (i - 1, worst()))
    o_ref[...] = _fold_heads(acc_ref[0], acc_ref[1], first)


def _strict_upper(n):
    r = jnp.arange(n)
    return (r[:, None] > r[None, :]).astype(BF16)


def _sb_attention(q, k, v, batch, seq):
    t = SB_TILE
    assert seq % t == 0
    nq = seq // t
    n_pairs = N_SB_HEADS // HEADS_PER_VREG
    u = _strict_upper(t)
    return pl.pallas_call(
        _sb_kernel,
        grid=(batch, n_pairs, nq),
        in_specs=[
            pl.BlockSpec((t, LANES), lambda b, hp, i: (b * nq + i, hp)),
            pl.BlockSpec((seq, LANES), lambda b, hp, i: (b, hp)),
            pl.BlockSpec((seq, LANES), lambda b, hp, i: (b, hp)),
            pl.BlockSpec((2 * t, t), lambda b, hp, i: (0, 0)),
        ],
        out_specs=pl.BlockSpec((t, LANES), lambda b, hp, i: (b * nq + i, hp)),
        out_shape=jax.ShapeDtypeStruct((batch * seq, SB_WIDTH), F32),
        scratch_shapes=[
            pltpu.VMEM((HEADS_PER_VREG, seq, 2 * LANES), BF16),
            pltpu.VMEM((HEADS_PER_VREG, nq, 2 * t, LANES), BF16),
            pltpu.VMEM((HEADS_PER_VREG, t, LANES), F32),
            pltpu.VMEM((HEADS_PER_VREG, t, 1), F32),
        ],
        compiler_params=_cparams(("parallel", "arbitrary", "arbitrary")),
        name="sb_attention",
    )(q, k, v, jnp.concatenate([u, u], axis=0))


def _top_blocks(gate, pos, allowed, lowest_pos_wins, axis):
    g = jnp.where(allowed, gate, -jnp.inf)
    sel = jnp.zeros(gate.shape, F32)
    for _ in range(MOBA_TOPK):
        m = jnp.max(g, axis=axis, keepdims=True)
        hit = g == m
        if lowest_pos_wins:
            idx = jnp.min(jnp.where(hit, pos, jnp.int32(2 ** 30)), axis=axis, keepdims=True)
        else:
            idx = jnp.max(jnp.where(hit, pos, jnp.int32(-1)), axis=axis, keepdims=True)
        at = pos == idx
        sel = jnp.where(jnp.logical_and(at, m > -jnp.inf), 1.0, sel)
        g = jnp.where(at, -jnp.inf, g)
    return sel


def _moba_kernel(q_ref, k_ref, v_ref, o_ref, kmean_ref, kcat_ref, vcat_ref, m_ref, l_ref, acc_ref, *, n_blocks):
    t = MOBA_BLOCK
    i = pl.program_id(2)
    lane = lax.broadcasted_iota(jnp.int32, (t, LANES), 1)
    first = lane < HEAD_DIM

    @pl.when(i == 0)
    def _():
        kmean_ref[...] = jnp.zeros_like(kmean_ref)

        def build(n, c):
            rows = pl.ds(pl.multiple_of(n * t, t), t)
            k = k_ref[rows, :]
            _put_row(kmean_ref, n, jnp.mean(k, axis=0, keepdims=True))
            for hh, kc in enumerate(_stack_keys(k, first)):
                kcat_ref[hh, rows, :] = kc
            for hh, (hi_lo, _) in enumerate(_stack_values(v_ref[rows, :], first)):
                vcat_ref[hh, rows, :] = hi_lo.astype(BF16)
            return c
        lax.fori_loop(0, n_blocks, build, 0)

    q = q_ref[...]
    q_heads = _stack_queries(q * ATT_SCALE, first)
    q_f32 = (jnp.where(first, q, 0.0), jnp.where(first, 0.0, q))
    kmean = kmean_ref[...]
    bit = jnp.left_shift(1, jnp.minimum(lane, MAX_MOBA_BLOCKS)).astype(F32)
    picked = []
    for hh in range(HEADS_PER_VREG):
        sel = _top_blocks(_dot_f32(q_f32[hh], kmean, nt=True), lane, lane < i, True, 1)
        picked.append(jnp.sum(sel * bit, axis=-1, keepdims=True).astype(jnp.int32))

    def visit(j, keep):
        rows = pl.ds(pl.multiple_of(j * t, t), t)
        for hh in range(HEADS_PER_VREG):
            s = jnp.where(keep[hh], _dot_nt(q_heads[hh], kcat_ref[hh, rows, :]), NEG_BIG)
            m = m_ref[hh]
            m_new = jnp.maximum(m, jnp.max(s, axis=-1, keepdims=True))
            alpha = jnp.exp(m - m_new)
            p = jnp.exp(s - m_new)
            m_ref[hh] = m_new
            l_ref[hh] = alpha * l_ref[hh] + jnp.sum(p, axis=-1, keepdims=True)
            p_hi, p_lo = _split(p, 2)
            vc = vcat_ref[hh, rows, :]
            acc_ref[hh] = alpha * acc_ref[hh] + (_dot(p_hi, vc) + _dot(p_lo, vc))

    m_ref[...] = jnp.full(m_ref.shape, NEG_BIG, F32)
    l_ref[...] = jnp.zeros_like(l_ref)
    acc_ref[...] = jnp.zeros_like(acc_ref)
    causal = (lax.broadcasted_iota(jnp.int32, (t, t), 1) <= lax.broadcasted_iota(jnp.int32, (t, t), 0))
    visit(i, (causal, causal))

    def body(j, c):
        visit(j, [jnp.bitwise_and(jnp.right_shift(picked[hh], j), 1) > 0 for hh in range(HEADS_PER_VREG)])
        return c

    lax.fori_loop(0, i, body, 0)
    o_ref[...] = _fold_heads(acc_ref[0] / l_ref[0], acc_ref[1] / l_ref[1], first)


def _moba_attention(q, k, v, batch, seq):
    t = MOBA_BLOCK
    assert seq % t == 0 and seq // t <= MAX_MOBA_BLOCKS
    nq = seq // t
    n_pairs = N_MOBA_HEADS // HEADS_PER_VREG
    first_pair = N_SB_HEADS // HEADS_PER_VREG
    q_map = lambda b, hp, i: (b * nq + i, first_pair + hp)
    kv_map = lambda b, hp, i: (b, first_pair + hp)
    return pl.pallas_call(
        functools.partial(_moba_kernel, n_blocks=nq),
        grid=(batch, n_pairs, nq),
        in_specs=[
            pl.BlockSpec((t, LANES), q_map),
            pl.BlockSpec((seq, LANES), kv_map),
            pl.BlockSpec((seq, LANES), kv_map),
        ],
        out_specs=pl.BlockSpec((t, LANES), lambda b, hp, i: (b * nq + i, hp)),
        out_shape=jax.ShapeDtypeStruct((batch * seq, MOBA_WIDTH), F32),
        scratch_shapes=[
            pltpu.VMEM((LANES, LANES), F32),
            pltpu.VMEM((HEADS_PER_VREG, seq, 2 * LANES), BF16),
            pltpu.VMEM((HEADS_PER_VREG, seq, LANES), BF16),
            pltpu.VMEM((HEADS_PER_VREG, t, 1), F32),
            pltpu.VMEM((HEADS_PER_VREG, t, 1), F32),
            pltpu.VMEM((HEADS_PER_VREG, t, LANES), F32),
        ],
        compiler_params=_cparams(("parallel", "arbitrary", "arbitrary")),
        name="moba_attention",
    )(q, k, v)


def _put_row(ref, p, row):
    base = pl.multiple_of((p // SUBLANES) * SUBLANES, SUBLANES)
    cur = ref[pl.ds(base, SUBLANES), :]
    sub = lax.broadcasted_iota(jnp.int32, cur.shape, 0)
    ref[pl.ds(base, SUBLANES), :] = jnp.where(sub == p % SUBLANES, row, cur)


def _put_col(ref, p, col):
    cur = ref[...]
    lane = lax.broadcasted_iota(jnp.int32, cur.shape, 1)
    ref[...] = jnp.where(lane == p, col, cur)


def _sample_attn_kernel(pt_ref, qrow_ref, knew_ref, vnew_ref, k0_ref, k1_ref, v0_ref, v1_ref, u_ref,
                        o_ref, carry_ref, osb_ref, m_ref, l_ref, ks_ref, omb_ref, *, n_blocks, n_new):
    del pt_ref
    p = pl.program_id(1)
    group_rows = n_new * N_SB_HEADS
    n_rows = 2 * group_rows

    @pl.when(p == 0)
    def _():
        carry_ref[...] = jnp.zeros_like(carry_ref)
        osb_ref[...] = jnp.zeros_like(osb_ref)
        m_ref[...] = jnp.full(m_ref.shape, NEG_BIG, F32)
        l_ref[...] = jnp.zeros_like(l_ref)
        ks_ref[...] = jnp.zeros_like(ks_ref)
        omb_ref[...] = jnp.zeros_like(omb_ref)

    head_diag = (lax.broadcasted_iota(jnp.int32, (N_MOBA_HEADS, MOBA_WIDTH), 1) // HEAD_DIM
                 == lax.broadcasted_iota(jnp.int32, (N_MOBA_HEADS, MOBA_WIDTH), 0)).astype(F32)

    def process(kt, vt, fresh):
        keys = kt.shape[1]
        z = _dot3(_split(qrow_ref[0] * ATT_SCALE, 2), _split(kt, 2))
        row = lax.broadcasted_iota(jnp.int32, (n_rows, keys), 0)
        sp = _softplus(z)
        log_keep = -sp
        s_mb = z
        if fresh:
            key = lax.broadcasted_iota(jnp.int32, (n_rows, keys), 1)
            tok = (row % group_rows) // N_SB_HEADS
            ok_sb = key < tok
            log_keep = jnp.where(ok_sb, log_keep, 0.0)
            s_mb = jnp.where(key <= tok, z, NEG_BIG)
        u = u_ref[0:keys, 0:keys]
        carry = carry_ref[...]
        log_between = _dot_split_lhs(log_keep, u, 2) + carry
        w_sb = jnp.exp(z - sp + log_between)
        if fresh:
            w_sb = jnp.where(ok_sb, w_sb, 0.0)
        carry_ref[...] = carry + jnp.sum(log_keep, axis=1, keepdims=True)
        m_blk = jnp.max(s_mb, axis=1, keepdims=True)
        e = jnp.exp(s_mb - m_blk)
        _put_col(m_ref, p, m_blk)
        _put_col(l_ref, p, jnp.sum(e, axis=1, keepdims=True))
        w = _split(jnp.where(row < group_rows, w_sb, e), 2)
        vb = _split(vt, 2)
        half = lambda parts, lo, hi: tuple(part[lo:hi] for part in parts)
        osb_ref[...] += _dot3(half(w, 0, group_rows), half(vb, 0, SB_WIDTH), nt=True)
        o_blk = _dot3(half(w, group_rows, n_rows), half(vb, SB_WIDTH, ATT_WIDTH), nt=True)
        for tkn in range(n_new):
            part = o_blk[tkn * N_MOBA_HEADS:(tkn + 1) * N_MOBA_HEADS] * head_diag
            _put_row(omb_ref.at[tkn], p, jnp.sum(part, axis=0, keepdims=True))
        _put_col(ks_ref, p, jnp.sum(kt[SB_WIDTH:], axis=1, keepdims=True))

    @pl.when(p == 0)
    def _():
        process(knew_ref[0], vnew_ref[0], True)

    @pl.when(p > 0)
    def _():
        process(jnp.concatenate([k0_ref[0], k1_ref[0]], axis=1),
                jnp.concatenate([v0_ref[0], v1_ref[0]], axis=1), False)

    @pl.when(p == n_blocks)
    def _():
        lane = lax.broadcasted_iota(jnp.int32, (n_rows, LANES), 1)
        gate = _dot_f32(qrow_ref[0][:, SB_WIDTH:], ks_ref[...])
        past = jnp.logical_and(lane >= 1, lane <= n_blocks)
        sel = _top_blocks(gate, lane, past, False, 1)
        sel = jnp.logical_or(sel > 0.5, lane == 0)
        m_all = m_ref[...]
        m_top = jnp.max(jnp.where(sel, m_all, NEG_BIG), axis=1, keepdims=True)
        coef = jnp.where(sel, jnp.exp(m_all - m_top), 0.0)
        coef = coef / jnp.sum(coef * l_ref[...], axis=1, keepdims=True)
        coef_t = jnp.concatenate([coef, jnp.zeros((LANES - n_rows, LANES), F32)], axis=0).T
        coef_t = coef_t[0:omb_ref.shape[1]]
        col_head = lax.broadcasted_iota(jnp.int32, (LANES, MOBA_WIDTH), 1) // HEAD_DIM
        src_row = lax.broadcasted_iota(jnp.int32, (LANES, MOBA_WIDTH), 0)
        osb = osb_ref[...]
        for tkn in range(n_new):
            spread = (src_row == group_rows + tkn * N_MOBA_HEADS + col_head).astype(BF16)
            weighted = _dot_split_lhs(coef_t, spread, 3) * omb_ref[tkn]
            o_ref[0, tkn:tkn + 1, SB_WIDTH:] = jnp.sum(weighted, axis=0, keepdims=True)
            part = osb[tkn * N_SB_HEADS:(tkn + 1) * N_SB_HEADS] * head_diag
            o_ref[0, tkn:tkn + 1, 0:SB_WIDTH] = jnp.sum(part, axis=0, keepdims=True)


def _sample_attention(q, k_new, v_new, cache_kt, cache_vt, page_table):
    n_seq, n_new, _ = q.shape
    _, _, page = cache_kt.shape
    n_pages = page_table.shape[1]
    assert MOBA_BLOCK == 2 * page and page == LANES and n_pages % 2 == 0 and n_new <= SUBLANES
    assert N_SB_HEADS == N_MOBA_HEADS and 2 * n_new * N_SB_HEADS <= LANES
    n_blocks = n_pages // 2
    steps = n_blocks + 1
    assert steps <= LANES
    steps_pad = -(-steps // BF16_ROWS) * BF16_ROWS
    n_rows = 2 * n_new * N_SB_HEADS

    qh = q.reshape(n_seq, n_new, 2, N_SB_HEADS, HEAD_DIM)
    eye_g = jnp.eye(2, dtype=F32)
    eye_h = jnp.eye(N_SB_HEADS, dtype=F32)
    qrow = jnp.einsum("btghd,gG,hH->bGtHghd", qh, eye_g, eye_h).reshape(n_seq, n_rows, ATT_WIDTH)
    new_t = lambda a: jnp.pad(jnp.swapaxes(a, 1, 2), ((0, 0), (0, 0), (0, page - n_new)))

    def page_map(which):
        def index(b, p, pt):
            blk = n_blocks - jnp.maximum(p, 1)
            return (pt[b * n_pages + 2 * blk + which], 0, 0)
        return index

    seq_map = lambda b, p, pt: (b, 0, 0)
    page_spec = lambda which: pl.BlockSpec((1, ATT_WIDTH, page), page_map(which))
    grid_spec = pltpu.PrefetchScalarGridSpec(
        num_scalar_prefetch=1,
        grid=(n_seq, steps),
        in_specs=[
            pl.BlockSpec((1, n_rows, ATT_WIDTH), seq_map),
            pl.BlockSpec((1, ATT_WIDTH, page), seq_map),
            pl.BlockSpec((1, ATT_WIDTH, page), seq_map),
            page_spec(0), page_spec(1), page_spec(0), page_spec(1),
            pl.BlockSpec((MOBA_BLOCK, MOBA_BLOCK), lambda b, p, pt: (0, 0)),
        ],
        out_specs=pl.BlockSpec((1, n_new, ATT_WIDTH), seq_map),
        scratch_shapes=[
            pltpu.VMEM((n_rows, 1), F32),
            pltpu.VMEM((n_new * N_SB_HEADS, SB_WIDTH), F32),
            pltpu.VMEM((n_rows, LANES), F32),
            pltpu.VMEM((n_rows, LANES), F32),
            pltpu.VMEM((MOBA_WIDTH, LANES), F32),
            pltpu.VMEM((n_new, steps_pad, MOBA_WIDTH), F32),
        ],
    )
    return pl.pallas_call(
        functools.partial(_sample_attn_kernel, n_blocks=n_blocks, n_new=n_new),
        grid_spec=grid_spec,
        out_shape=jax.ShapeDtypeStruct((n_seq, n_new, ATT_WIDTH), F32),
        compiler_params=_cparams(("parallel", "arbitrary")),
        name="sample_attention",
    )(page_table.reshape(-1), qrow, new_t(k_new), new_t(v_new),
      cache_kt, cache_kt, cache_vt, cache_vt, _strict_upper(MOBA_BLOCK))


def _attn_out_kernel(x_ref, osb_ref, omb_ref, wsb_hi_ref, wsb_lo_ref, wmb_hi_ref, wmb_lo_ref, y_ref):
    sb = _dot3(_split(osb_ref[...], 2), (wsb_hi_ref[...], wsb_lo_ref[...]))
    mb = _dot3(_split(omb_ref[...], 2), (wmb_hi_ref[...], wmb_lo_ref[...]))
    y_ref[...] = x_ref[...] + (sb + mb)


def _attn_out(x2d, o_sb, o_mb, w_o_parts):
    n, d = x2d.shape
    tm = _row_tile(n, 512)
    row = lambda i: (i, 0)
    fixed = lambda i: (0, 0)
    return pl.pallas_call(
        _attn_out_kernel,
        grid=(n // tm,),
        in_specs=[
            pl.BlockSpec((tm, d), row),
            pl.BlockSpec((tm, SB_WIDTH), row),
            pl.BlockSpec((tm, MOBA_WIDTH), row),
            pl.BlockSpec((SB_WIDTH, d), fixed),
            pl.BlockSpec((SB_WIDTH, d), fixed),
            pl.BlockSpec((MOBA_WIDTH, d), lambda i: (1, 0)),
            pl.BlockSpec((MOBA_WIDTH, d), lambda i: (1, 0)),
        ],
        out_specs=pl.BlockSpec((tm, d), row),
        out_shape=jax.ShapeDtypeStruct((n, d), F32),
        compiler_params=_cparams(("parallel",)),
        name="attn_out_proj",
    )(x2d, o_sb, o_mb, w_o_parts[0], w_o_parts[1], w_o_parts[0], w_o_parts[1])


def _pool_kernel(*refs, pos0, rows, has_prev):
    if has_prev:
        x_ref, xprev_ref, pre_ref, g_ref, w_ref, sc_ref, y_ref, tail_ref = refs
    else:
        x_ref, pre_ref, g_ref, w_ref, sc_ref, y_ref, tail_ref = refs
    ti = pl.program_id(1)
    g = g_ref[...]
    x = x_ref[0]
    d = x.shape[-1]
    h = _rms(x, g)
    halo = pre_ref[0]
    if has_prev:
        halo = jnp.where(ti == 0, halo, _rms(xprev_ref[0], g))
    pad = (-rows) % SUBLANES
    pieces = [halo, h] + ([jnp.zeros((pad, d), F32)] if pad else [])
    full = jnp.concatenate(pieces, axis=0)
    tail_ref[0] = full[rows:rows + POOL_HALO]

    group = d // len(POOL_WINDOWS)
    pos = pos0 + ti * rows + lax.broadcasted_iota(jnp.int32, (rows, 1), 0)
    s = full
    ys = []
    for gi, win in enumerate(POOL_WINDOWS):
        s = s[:, (group if gi else 0):]
        s = s + pltpu.roll(s, win // 2, 0)
        count = jnp.minimum(win, pos + 1).astype(F32)
        diff = s[POOL_HALO:POOL_HALO + rows, :group] / count - h[:, gi * group:(gi + 1) * group]
        ys.append(_dot(diff.astype(BF16), w_ref[gi]))
    y_ref[0] = x + jnp.concatenate(ys, axis=-1) * sc_ref[...]


def _pool_mixer(x, prefix_h, g, w_pool_bf16, scale, pos0):
    b, t, d = x.shape
    rows = min(t, 512)
    assert t % rows == 0 and rows % POOL_HALO == 0 or t == rows
    has_prev = t > rows
    nt = t // rows
    per = rows // POOL_HALO if has_prev else 1
    in_specs = [pl.BlockSpec((1, rows, d), lambda bi, ti: (bi, ti, 0))]
    args = [x]
    if has_prev:
        in_specs.append(pl.BlockSpec((1, POOL_HALO, d), lambda bi, ti: (bi, jnp.maximum(ti * per - 1, 0), 0)))
        args.append(x)
    fixed2 = lambda bi, ti: (0, 0)
    in_specs += [
        pl.BlockSpec((1, POOL_HALO, d), lambda bi, ti: (bi, 0, 0)),
        pl.BlockSpec((1, d), fixed2),
        pl.BlockSpec(w_pool_bf16.shape, lambda bi, ti: (0, 0, 0)),
        pl.BlockSpec((1, d), fixed2),
    ]
    args += [prefix_h, g.reshape(1, d), w_pool_bf16, scale.reshape(1, d)]
    return pl.pallas_call(
        functools.partial(_pool_kernel, pos0=pos0, rows=rows, has_prev=has_prev),
        grid=(b, nt),
        in_specs=in_specs,
        out_specs=[
            pl.BlockSpec((1, rows, d), lambda bi, ti: (bi, ti, 0)),
            pl.BlockSpec((1, POOL_HALO, d), lambda bi, ti: (bi, 0, 0)),
        ],
        out_shape=[jax.ShapeDtypeStruct((b, t, d), F32), jax.ShapeDtypeStruct((b, POOL_HALO, d), F32)],
        compiler_params=_cparams(("parallel", "arbitrary")),
        name="pool_mixer",
    )(*args)


def _router_kernel(x_ref, g_ref, wr_ref, xn_ref, comb_ref):
    xn = _rms(x_ref[...], g_ref[...])
    xn_ref[...] = xn.astype(BF16)
    logits = _dot_f32(xn, wr_ref[...])
    lane = lax.broadcasted_iota(jnp.int32, logits.shape, 1)
    is_group = jnp.logical_and(lane >= N_EXPERTS, lane < N_EXPERTS + N_EXPERT_GROUPS)
    glog = jnp.where(is_group, logits, -jnp.inf)
    gmax = jnp.max(glog, axis=-1, keepdims=True)
    gsum = jnp.sum(jnp.exp(glog - gmax), axis=-1, keepdims=True)
    g_val = 1.0 / gsum
    g_idx = jnp.min(jnp.where(glog == gmax, lane, 2 * LANES), axis=-1, keepdims=True) - N_EXPERTS
    in_group = jnp.logical_and(lane >= g_idx * EXPERTS_PER_GROUP, lane < (g_idx + 1) * EXPERTS_PER_GROUP)
    elog = jnp.where(in_group, logits, -jnp.inf)
    eexp = jnp.exp(elog - jnp.max(elog, axis=-1, keepdims=True))
    prob = eexp / jnp.sum(eexp, axis=-1, keepdims=True)
    p1 = jnp.max(prob, axis=-1, keepdims=True)
    i1 = jnp.min(jnp.where(jnp.logical_and(in_group, prob == p1), lane, 2 * LANES), axis=-1, keepdims=True)
    rest = jnp.where(jnp.logical_and(in_group, lane != i1), prob, -1.0)
    p2 = jnp.max(rest, axis=-1, keepdims=True)
    i2 = jnp.min(jnp.where(rest == p2, lane, 2 * LANES), axis=-1, keepdims=True)
    norm = g_val / (p1 + p2)
    comb_ref[...] = jnp.where(lane == i1, p1 * norm, 0.0) + jnp.where(lane == i2, p2 * norm, 0.0)


def _router(x2d, g, w_router):
    n, d = x2d.shape
    tm = _row_tile(n, 512)
    row = lambda i: (i, 0)
    fixed = lambda i: (0, 0)
    return pl.pallas_call(
        _router_kernel,
        grid=(n // tm,),
        in_specs=[pl.BlockSpec((tm, d), row), pl.BlockSpec((1, d), fixed), pl.BlockSpec((d, LANES), fixed)],
        out_specs=[pl.BlockSpec((tm, d), row), pl.BlockSpec((tm, LANES), row)],
        out_shape=[jax.ShapeDtypeStruct((n, d), BF16), jax.ShapeDtypeStruct((n, LANES), F32)],
        compiler_params=_cparams(("parallel",)),
        name="ffn_router",
    )(x2d, g.reshape(1, d), w_router)


def _moe_kernel(x_ref, xn_ref, comb_ref, wg_ref, wu_ref, wd_ref, y_ref):
    e = pl.program_id(1)

    @pl.when(e == 0)
    def _():
        y_ref[...] = x_ref[...]

    xn = xn_ref[...]
    gate = _dot(xn, wg_ref[0])
    up = _dot(xn, wu_ref[0])
    ff = gate.shape[-1]
    pick = (lax.broadcasted_iota(jnp.int32, (LANES, ff), 0) == e).astype(BF16)
    weight = _dot_split_lhs(comb_ref[...], pick, 3)
    hidden = gate * jax.nn.sigmoid(gate) * up * weight
    y_ref[...] += _dot(hidden.astype(BF16), wd_ref[0])


def _moe(x2d, xn, comb, wg, wu, wd):
    n, d = x2d.shape
    n_exp, _, ff = wg.shape
    tm = _row_tile(n, 1024)
    row = lambda i, e: (i, 0)
    return pl.pallas_call(
        _moe_kernel,
        grid=(n // tm, n_exp),
        in_specs=[
            pl.BlockSpec((tm, d), row),
            pl.BlockSpec((tm, d), row),
            pl.BlockSpec((tm, LANES), row),
            pl.BlockSpec((1, d, ff), lambda i, e: (e, 0, 0)),
            pl.BlockSpec((1, d, ff), lambda i, e: (e, 0, 0)),
            pl.BlockSpec((1, ff, d), lambda i, e: (e, 0, 0)),
        ],
        out_specs=pl.BlockSpec((tm, d), row),
        out_shape=jax.ShapeDtypeStruct((n, d), F32),
        compiler_params=_cparams(("parallel", "arbitrary")),
        name="moe_experts",
    )(x2d, xn, comb, wg, wu, wd)


def _router_weight(w_group, w_expert):
    d = w_group.shape[0]
    w = jnp.concatenate([w_expert, w_group], axis=1)
    return jnp.pad(w, ((0, 0), (0, LANES - w.shape[1]))).astype(F32).reshape(d, LANES)


def _cache_pages_t(cache):
    layers, pages, page, heads, dim = cache.shape
    return jnp.transpose(cache, (0, 1, 3, 4, 2)).reshape(layers * pages, heads * dim, page)


def _ffn(x2d, g, w_router, wg, wu, wd):
    xn, comb = _router(x2d, g, w_router)
    return _moe(x2d, xn, comb, wg, wu, wd)


def kernel(x_prompt, x_sample, cache_k, cache_v, state_pool, page_table, norm_mix_g, norm_ffn_g, att_w_qkv, att_q_norm_g, att_k_norm_g, att_w_o, pool_w, pool_scale, moe_w_router_group, moe_w_router_expert, moe_w_gate, moe_w_up, moe_w_down):
    batch, seq, d = x_prompt.shape
    n_seq, n_new, _ = x_sample.shape
    depth = norm_mix_g.shape[0]
    page = cache_k.shape[2]
    n_past = page_table.shape[1] * page
    assert depth == 2 and d == ATT_WIDTH and n_past % MOBA_BLOCK == 0

    w_qkv = _split_weights(att_w_qkv[0])
    w_o = _split_weights(att_w_o[0])
    w_pool = pool_w.astype(BF16)
    wg, wu, wd = moe_w_gate.astype(BF16), moe_w_up.astype(BF16), moe_w_down.astype(BF16)
    w_router = [_router_weight(moe_w_router_group[l], moe_w_router_expert[l]) for l in range(depth)]
    ffn = lambda x2d, l: _ffn(x2d, norm_ffn_g[l], w_router[l], wg[l], wu[l], wd[l])

    xp = x_prompt.reshape(batch * seq, d)
    xs = x_sample.reshape(n_seq * n_new, d)

    qg, kg = att_q_norm_g[0].reshape(-1), att_k_norm_g[0].reshape(-1)
    q_p, k_p, v_p = _qkv(xp, norm_mix_g[0], w_qkv, qg, kg)
    o_sb = _sb_attention(q_p, k_p, v_p, batch, seq)
    o_mb = _moba_attention(q_p, k_p, v_p, batch, seq)
    xp = _attn_out(xp, o_sb, o_mb, w_o)

    q_s, k_s, v_s = _qkv(xs, norm_mix_g[0], w_qkv, qg, kg)
    o_s = _sample_attention(
        q_s.reshape(n_seq, n_new, d), k_s.reshape(n_seq, n_new, d), v_s.reshape(n_seq, n_new, d),
        _cache_pages_t(cache_k), _cache_pages_t(cache_v), page_table)
    o_s = o_s.reshape(n_seq * n_new, d)
    xs = _attn_out(xs, o_s[:, :SB_WIDTH], o_s[:, SB_WIDTH:], w_o)

    xp = ffn(xp, 0)
    xs = ffn(xs, 0)

    zeros_p = jnp.zeros((batch, POOL_HALO, d), F32)
    xp3, tail_p = _pool_mixer(xp.reshape(batch, seq, d), zeros_p, norm_mix_g[1], w_pool[0], pool_scale[0], 0)
    prefix_s = jnp.pad(state_pool[0], ((0, 0), (POOL_HALO - POOL_STATE, 0), (0, 0)))
    xs3, tail_s = _pool_mixer(xs.reshape(n_seq, n_new, d), prefix_s, norm_mix_g[1], w_pool[0], pool_scale[0], n_past)
    xp = ffn(xp3.reshape(batch * seq, d), 1)
    xs = ffn(xs3.reshape(n_seq * n_new, d), 1)

    heads = (N_ATT_HEADS, HEAD_DIM)
    return (
        xp.reshape(batch, seq, d),
        xs.reshape(n_seq, n_new, d),
        k_p.reshape(1, batch, seq, *heads),
        v_p.reshape(1, batch, seq, *heads),
        k_s.reshape(1, n_seq, n_new, *heads),
        v_s.reshape(1, n_seq, n_new, *heads),
        tail_p[None, :, POOL_HALO - POOL_STATE:],
        tail_s[None, :, POOL_HALO - POOL_STATE:],
    )
```

```python
import functools

import jax
import jax.numpy as jnp
from jax import lax
from jax.experimental import pallas as pl
from jax.experimental.pallas import tpu as pltpu

F32 = jnp.float32
BF16 = jnp.bfloat16

HEAD_DIM = 64
N_SB_HEADS = 8
N_MOBA_HEADS = 8
N_ATT_HEADS = N_SB_HEADS + N_MOBA_HEADS
ATT_WIDTH = N_ATT_HEADS * HEAD_DIM
SB_WIDTH = N_SB_HEADS * HEAD_DIM
MOBA_WIDTH = N_MOBA_HEADS * HEAD_DIM
ATT_SCALE = HEAD_DIM ** -0.5
MOBA_BLOCK = 256
MOBA_TOPK = 3
MAX_MOBA_BLOCKS = 24
POOL_WINDOWS = (2, 4, 8, 16)
POOL_STATE = max(POOL_WINDOWS) - 1
POOL_HALO = POOL_STATE + 1
N_EXPERT_GROUPS = 4
EXPERTS_PER_GROUP = 8
N_EXPERTS = N_EXPERT_GROUPS * EXPERTS_PER_GROUP
RMS_EPS = 1e-6

LANES = 128
SUBLANES = 8
BF16_ROWS = 16
HEADS_PER_VREG = LANES // HEAD_DIM
SB_TILE = 128
NEG_BIG = -1e30
SB_LOG_CUTOFF = -104.0
VMEM_LIMIT = 48 * 1024 * 1024


def _cparams(sem):
    return pltpu.CompilerParams(dimension_semantics=sem, vmem_limit_bytes=VMEM_LIMIT)


def _dot(a, b):
    return jnp.dot(a, b, preferred_element_type=F32)


def _dot_nt(a, b):
    return lax.dot_general(a, b, (((1,), (1,)), ((), ())), preferred_element_type=F32)


def _split(a, n):
    parts = []
    for _ in range(n - 1):
        p = a.astype(BF16)
        parts.append(p)
        a = a - p.astype(F32)
    parts.append(a.astype(BF16))
    return parts


def _split_weights(w):
    hi = lax.bitcast_convert_type(lax.bitcast_convert_type(w, jnp.uint32) & jnp.uint32(0xFFFF0000), F32)
    return hi.astype(BF16), (w - hi).astype(BF16)


def _dot_split_lhs(a, b_exact, n):
    out = None
    for p in _split(a, n):
        t = _dot(p, b_exact)
        out = t if out is None else out + t
    return out


def _dot_split_rhs(a_exact, b, n):
    out = None
    for p in _split(b, n):
        t = _dot(a_exact, p)
        out = t if out is None else out + t
    return out


def _dot3(a, b, nt=False):
    d = _dot_nt if nt else _dot
    return d(a[0], b[0]) + (d(a[1], b[0]) + d(a[0], b[1]))


def _dot_f32(a, b, nt=False):
    a1, a2, a3 = _split(a, 3)
    b1, b2, b3 = _split(b, 3)
    d = _dot_nt if nt else _dot
    return (d(a1, b1) + (d(a1, b2) + d(a2, b1))) + (d(a1, b3) + d(a3, b1) + d(a2, b2))


def _rms(x, g):
    return x * lax.rsqrt(jnp.mean(x * x, axis=-1, keepdims=True) + RMS_EPS) * g


def _softplus(z):
    return jnp.maximum(z, 0.0) + jnp.log1p(jnp.exp(-jnp.abs(z)))


def _row_tile(n, want):
    t = min(n, want)
    assert n % t == 0, (n, t)
    return t


def _qkv_kernel(x_ref, g_ref, whi_ref, wlo_ref, qg_ref, kg_ref, seg_ref, segt_ref, q_ref, k_ref, v_ref):
    h = _rms(x_ref[...], g_ref[...])
    qkv = _dot3(_split(h, 2), (whi_ref[...], wlo_ref[...]))

    def head_norm(t, gain):
        ssq = _dot_split_lhs(t * t, seg_ref[...], 2)
        inv = lax.rsqrt(ssq * (1.0 / HEAD_DIM) + RMS_EPS)
        return t * _dot_split_lhs(inv, segt_ref[...], 2) * gain

    q = head_norm(qkv[:, :ATT_WIDTH], qg_ref[...])
    k = head_norm(qkv[:, ATT_WIDTH:2 * ATT_WIDTH], kg_ref[...])
    v = qkv[:, 2 * ATT_WIDTH:]
    q_ref[...] = q
    k_ref[...] = k
    v_ref[...] = v


def _qkv(x2d, g, w_parts, qg, kg):
    n, d = x2d.shape
    tm = _row_tile(n, 256)
    head_of_col = jnp.arange(ATT_WIDTH) // HEAD_DIM
    seg = (head_of_col[:, None] == jnp.arange(LANES)[None, :]).astype(BF16)
    segt = seg.T
    row = lambda i: (i, 0)
    fixed = lambda i: (0, 0)
    return pl.pallas_call(
        _qkv_kernel,
        grid=(n // tm,),
        in_specs=[
            pl.BlockSpec((tm, d), row),
            pl.BlockSpec((1, d), fixed),
            pl.BlockSpec((d, 3 * ATT_WIDTH), fixed),
            pl.BlockSpec((d, 3 * ATT_WIDTH), fixed),
            pl.BlockSpec((1, ATT_WIDTH), fixed),
            pl.BlockSpec((1, ATT_WIDTH), fixed),
            pl.BlockSpec((ATT_WIDTH, LANES), fixed),
            pl.BlockSpec((LANES, ATT_WIDTH), fixed),
        ],
        out_specs=[pl.BlockSpec((tm, ATT_WIDTH), row)] * 3,
        out_shape=[jax.ShapeDtypeStruct((n, ATT_WIDTH), F32)] * 3,
        compiler_params=_cparams(("parallel",)),
        name="qkv_proj",
    )(x2d, g.reshape(1, d), w_parts[0], w_parts[1], qg.reshape(1, ATT_WIDTH), kg.reshape(1, ATT_WIDTH), seg, segt)


def _hi_lo(x):
    hi = x.astype(BF16).astype(F32)
    return hi, x - hi


def _swap_heads(x):
    return pltpu.roll(x, HEAD_DIM, 1)


def _stack_queries(q, first):
    hi, lo = _hi_lo(q)
    hs, ls = _swap_heads(hi), _swap_heads(lo)
    head_a = jnp.concatenate([jnp.where(first, hi, ls), jnp.where(first, hi, 0.0)], axis=1)
    head_b = jnp.concatenate([jnp.where(first, hs, lo), jnp.where(first, hs, 0.0)], axis=1)
    return head_a.astype(BF16), head_b.astype(BF16)


def _stack_keys(k, first):
    hi, lo = _hi_lo(k)
    hs, ls = _swap_heads(hi), _swap_heads(lo)
    head_a = jnp.concatenate([jnp.where(first, hi, hs), jnp.where(first, lo, 0.0)], axis=1)
    head_b = jnp.concatenate([jnp.where(first, hs, hi), jnp.where(first, ls, 0.0)], axis=1)
    return head_a.astype(BF16), head_b.astype(BF16)


def _stack_values(v, first):
    hi, lo = _hi_lo(v)
    hs, ls = _swap_heads(hi), _swap_heads(lo)
    return ((jnp.where(first, hi, ls), jnp.where(first, hi, 0.0)),
            (jnp.where(first, hs, lo), jnp.where(first, hs, 0.0)))


def _fold_heads(acc_a, acc_b, first):
    return jnp.where(first, acc_a + _swap_heads(acc_a), acc_b + _swap_heads(acc_b))


def _sb_kernel(q_ref, k_ref, v_ref, u2_ref, o_ref, kcat_ref, vcat_ref, acc_ref, carry_ref):
    t = SB_TILE
    i = pl.program_id(2)
    first = lax.broadcasted_iota(jnp.int32, (t, LANES), 1) < HEAD_DIM

    @pl.when(i == 0)
    def _():
        def build(j, c):
            rows = pl.ds(pl.multiple_of(j * t, t), t)
            for hh, kc in enumerate(_stack_keys(k_ref[rows, :], first)):
                kcat_ref[hh, rows, :] = kc
            for hh, (top, bottom) in enumerate(_stack_values(v_ref[rows, :], first)):
                vcat_ref[hh, j] = jnp.concatenate([top, bottom], axis=0).astype(BF16)
            return c
        lax.fori_loop(0, k_ref.shape[0] // t, build, 0)

    q_heads = _stack_queries(q_ref[...] * ATT_SCALE, first)
    strictly_past = (lax.broadcasted_iota(jnp.int32, (t, t), 1) < lax.broadcasted_iota(jnp.int32, (t, t), 0))
    acc_ref[...] = jnp.zeros_like(acc_ref)
    carry_ref[...] = jnp.zeros_like(carry_ref)

    def block(j, diagonal):
        rows = pl.ds(pl.multiple_of(j * t, t), t)
        for hh in range(HEADS_PER_VREG):
            z = _dot_nt(q_heads[hh], kcat_ref[hh, rows, :])
            sp = _softplus(z)
            log_keep = -sp
            if diagonal:
                log_keep = jnp.where(strictly_past, log_keep, 0.0)
            carry = carry_ref[hh]
            log_between = _dot(jnp.concatenate(_split(log_keep, 2), axis=1), u2_ref[...]) + carry
            w = jnp.exp(z - sp + log_between)
            if diagonal:
                w = jnp.where(strictly_past, w, 0.0)
            acc_ref[hh] += _dot(jnp.concatenate(_split(w, 2), axis=1), vcat_ref[hh, j])
            carry_ref[hh] = carry + jnp.sum(log_keep, axis=-1, keepdims=True)

    def worst():
        return jnp.max(jnp.maximum(carry_ref[0], carry_ref[1]))

    block(i, True)

    def cond(s):
        return jnp.logical_and(s[0] >= 1, s[1] > SB_LOG_CUTOFF)

    def body(s):
        block(s[0], False)
        block(s[0] - 1, False)
        return s[0] - 2, worst()

    j, live = lax.while_loop(cond, body, (i - 1, worst()))

    @pl.when(jnp.logical_and(j == 0, live > SB_LOG_CUTOFF))
    def _():
        block(0, False)

    o_ref[...] = _fold_heads(acc_ref[0], acc_ref[1], first)


def _strict_upper(n):
    r = jnp.arange(n)
    return (r[:, None] > r[None, :]).astype(BF16)


def _sb_attention(q, k, v, batch, seq):
    t = SB_TILE
    assert seq % t == 0
    nq = seq // t
    n_pairs = N_SB_HEADS // HEADS_PER_VREG
    u = _strict_upper(t)
    return pl.pallas_call(
        _sb_kernel,
        grid=(batch, n_pairs, nq),
        in_specs=[
            pl.BlockSpec((t, LANES), lambda b, hp, i: (b * nq + i, hp)),
            pl.BlockSpec((seq, LANES), lambda b, hp, i: (b, hp)),
            pl.BlockSpec((seq, LANES), lambda b, hp, i: (b, hp)),
            pl.BlockSpec((2 * t, t), lambda b, hp, i: (0, 0)),
        ],
        out_specs=pl.BlockSpec((t, LANES), lambda b, hp, i: (b * nq + i, hp)),
        out_shape=jax.ShapeDtypeStruct((batch * seq, SB_WIDTH), F32),
        scratch_shapes=[
            pltpu.VMEM((HEADS_PER_VREG, seq, 2 * LANES), BF16),
            pltpu.VMEM((HEADS_PER_VREG, nq, 2 * t, LANES), BF16),
            pltpu.VMEM((HEADS_PER_VREG, t, LANES), F32),
            pltpu.VMEM((HEADS_PER_VREG, t, 1), F32),
        ],
        compiler_params=_cparams(("parallel", "arbitrary", "arbitrary")),
        name="sb_attention",
    )(q, k, v, jnp.concatenate([u, u], axis=0))


def _top_blocks(gate, pos, allowed, lowest_pos_wins, axis):
    g = jnp.where(allowed, gate, -jnp.inf)
    sel = jnp.zeros(gate.shape, F32)
    for _ in range(MOBA_TOPK):
        m = jnp.max(g, axis=axis, keepdims=True)
        hit = g == m
        if lowest_pos_wins:
            idx = jnp.min(jnp.where(hit, pos, jnp.int32(2 ** 30)), axis=axis, keepdims=True)
        else:
            idx = jnp.max(jnp.where(hit, pos, jnp.int32(-1)), axis=axis, keepdims=True)
        at = pos == idx
        sel = jnp.where(jnp.logical_and(at, m > -jnp.inf), 1.0, sel)
        g = jnp.where(at, -jnp.inf, g)
    return sel


def _moba_kernel(q_ref, k_ref, v_ref, o_ref, kmean_ref, kcat_ref, vcat_ref, m_ref, l_ref, acc_ref, *, n_blocks):
    t = MOBA_BLOCK
    i = pl.program_id(2)
    lane = lax.broadcasted_iota(jnp.int32, (t, LANES), 1)
    first = lane < HEAD_DIM

    @pl.when(i == 0)
    def _():
        kmean_ref[...] = jnp.zeros_like(kmean_ref)

        def build(n, c):
            rows = pl.ds(pl.multiple_of(n * t, t), t)
            k = k_ref[rows, :]
            _put_row(kmean_ref, n, jnp.mean(k, axis=0, keepdims=True))
            for hh, kc in enumerate(_stack_keys(k, first)):
                kcat_ref[hh, rows, :] = kc
            for hh, (hi_lo, _) in enumerate(_stack_values(v_ref[rows, :], first)):
                vcat_ref[hh, rows, :] = hi_lo.astype(BF16)
            return c
        lax.fori_loop(0, n_blocks, build, 0)

    q = q_ref[...]
    q_heads = _stack_queries(q * ATT_SCALE, first)
    q_f32 = (jnp.where(first, q, 0.0), jnp.where(first, 0.0, q))
    kmean = kmean_ref[...]
    bit = jnp.left_shift(1, jnp.minimum(lane, MAX_MOBA_BLOCKS)).astype(F32)
    picked = []
    for hh in range(HEADS_PER_VREG):
        sel = _top_blocks(_dot_f32(q_f32[hh], kmean, nt=True), lane, lane < i, True, 1)
        picked.append(jnp.sum(sel * bit, axis=-1, keepdims=True).astype(jnp.int32))

    def visit(j, keep, n_visit=1):
        rows = pl.ds(pl.multiple_of(j * t, t), n_visit * t)
        for hh in range(HEADS_PER_VREG):
            s = jnp.where(keep[hh], _dot_nt(q_heads[hh], kcat_ref[hh, rows, :]), NEG_BIG)
            m = m_ref[hh]
            m_new = jnp.maximum(m, jnp.max(s, axis=-1, keepdims=True))
            alpha = jnp.exp(m - m_new)
            p = jnp.exp(s - m_new)
            m_ref[hh] = m_new
            l_ref[hh] = alpha * l_ref[hh] + jnp.sum(p, axis=-1, keepdims=True)
            p_hi, p_lo = _split(p, 2)
            vc = vcat_ref[hh, rows, :]
            acc_ref[hh] = alpha * acc_ref[hh] + (_dot(p_hi, vc) + _dot(p_lo, vc))

    m_ref[...] = jnp.full(m_ref.shape, NEG_BIG, F32)
    l_ref[...] = jnp.zeros_like(l_ref)
    acc_ref[...] = jnp.zeros_like(acc_ref)
    causal = (lax.broadcasted_iota(jnp.int32, (t, t), 1) <= lax.broadcasted_iota(jnp.int32, (t, t), 0))
    visit(i, (causal, causal))

    def took(hh, j):
        return jnp.broadcast_to(jnp.bitwise_and(jnp.right_shift(picked[hh], j), 1) > 0, (t, t))

    def body(jj, c):
        j = 2 * jj
        visit(j, [jnp.concatenate([took(hh, j), took(hh, j + 1)], axis=1) for hh in range(HEADS_PER_VREG)], 2)
        return c

    lax.fori_loop(0, i // 2, body, 0)

    @pl.when(i % 2 == 1)
    def _():
        visit(i - 1, [took(hh, i - 1) for hh in range(HEADS_PER_VREG)])
    o_ref[...] = _fold_heads(acc_ref[0] / l_ref[0], acc_ref[1] / l_ref[1], first)


def _moba_attention(q, k, v, batch, seq):
    t = MOBA_BLOCK
    assert seq % t == 0 and seq // t <= MAX_MOBA_BLOCKS
    nq = seq // t
    n_pairs = N_MOBA_HEADS // HEADS_PER_VREG
    first_pair = N_SB_HEADS // HEADS_PER_VREG
    q_map = lambda b, hp, i: (b * nq + i, first_pair + hp)
    kv_map = lambda b, hp, i: (b, first_pair + hp)
    return pl.pallas_call(
        functools.partial(_moba_kernel, n_blocks=nq),
        grid=(batch, n_pairs, nq),
        in_specs=[
            pl.BlockSpec((t, LANES), q_map),
            pl.BlockSpec((seq, LANES), kv_map),
            pl.BlockSpec((seq, LANES), kv_map),
        ],
        out_specs=pl.BlockSpec((t, LANES), lambda b, hp, i: (b * nq + i, hp)),
        out_shape=jax.ShapeDtypeStruct((batch * seq, MOBA_WIDTH), F32),
        scratch_shapes=[
            pltpu.VMEM((LANES, LANES), F32),
            pltpu.VMEM((HEADS_PER_VREG, seq, 2 * LANES), BF16),
            pltpu.VMEM((HEADS_PER_VREG, seq, LANES), BF16),
            pltpu.VMEM((HEADS_PER_VREG, t, 1), F32),
            pltpu.VMEM((HEADS_PER_VREG, t, 1), F32),
            pltpu.VMEM((HEADS_PER_VREG, t, LANES), F32),
        ],
        compiler_params=_cparams(("parallel", "arbitrary", "arbitrary")),
        name="moba_attention",
    )(q, k, v)


def _put_row(ref, p, row):
    base = pl.multiple_of((p // SUBLANES) * SUBLANES, SUBLANES)
    cur = ref[pl.ds(base, SUBLANES), :]
    sub = lax.broadcasted_iota(jnp.int32, cur.shape, 0)
    ref[pl.ds(base, SUBLANES), :] = jnp.where(sub == p % SUBLANES, row, cur)


def _put_col(ref, p, col):
    cur = ref[...]
    lane = lax.broadcasted_iota(jnp.int32, cur.shape, 1)
    ref[...] = jnp.where(lane == p, col, cur)


def _sample_attn_kernel(pt_ref, qrow_ref, knew_ref, vnew_ref, k0_ref, k1_ref, v0_ref, v1_ref, u_ref,
                        o_ref, carry_ref, osb_ref, m_ref, l_ref, ks_ref, omb_ref, *, n_blocks, n_new):
    del pt_ref
    p = pl.program_id(1)
    group_rows = n_new * N_SB_HEADS
    n_rows = 2 * group_rows

    @pl.when(p == 0)
    def _():
        carry_ref[...] = jnp.zeros_like(carry_ref)
        osb_ref[...] = jnp.zeros_like(osb_ref)
        m_ref[...] = jnp.full(m_ref.shape, NEG_BIG, F32)
        l_ref[...] = jnp.zeros_like(l_ref)
        ks_ref[...] = jnp.zeros_like(ks_ref)
        omb_ref[...] = jnp.zeros_like(omb_ref)

    head_diag = (lax.broadcasted_iota(jnp.int32, (N_MOBA_HEADS, MOBA_WIDTH), 1) // HEAD_DIM
                 == lax.broadcasted_iota(jnp.int32, (N_MOBA_HEADS, MOBA_WIDTH), 0)).astype(F32)

    def process(kt, vt, fresh):
        keys = kt.shape[1]
        z = _dot((qrow_ref[0] * ATT_SCALE).astype(BF16), kt.astype(BF16))
        row = lax.broadcasted_iota(jnp.int32, (n_rows, keys), 0)
        sp = _softplus(z)
        log_keep = -sp
        s_mb = z
        if fresh:
            key = lax.broadcasted_iota(jnp.int32, (n_rows, keys), 1)
            tok = (row % group_rows) // N_SB_HEADS
            ok_sb = key < tok
            log_keep = jnp.where(ok_sb, log_keep, 0.0)
            s_mb = jnp.where(key <= tok, z, NEG_BIG)
        u = u_ref[0:keys, 0:keys]
        carry = carry_ref[...]
        log_between = _dot_split_lhs(log_keep, u, 2) + carry
        w_sb = jnp.exp(z - sp + log_between)
        if fresh:
            w_sb = jnp.where(ok_sb, w_sb, 0.0)
        carry_ref[...] = carry + jnp.sum(log_keep, axis=1, keepdims=True)
        m_blk = jnp.max(s_mb, axis=1, keepdims=True)
        e = jnp.exp(s_mb - m_blk)
        _put_col(m_ref, p, m_blk)
        _put_col(l_ref, p, jnp.sum(e, axis=1, keepdims=True))
        w = jnp.where(row < group_rows, w_sb, e).astype(BF16)
        vb = vt.astype(BF16)
        osb_ref[...] += _dot_nt(w[0:group_rows], vb[0:SB_WIDTH])
        o_blk = _dot_nt(w[group_rows:n_rows], vb[SB_WIDTH:])
        for tkn in range(n_new):
            part = o_blk[tkn * N_MOBA_HEADS:(tkn + 1) * N_MOBA_HEADS] * head_diag
            _put_row(omb_ref.at[tkn], p, jnp.sum(part, axis=0, keepdims=True))
        _put_col(ks_ref, p, jnp.sum(kt[SB_WIDTH:], axis=1, keepdims=True))

    @pl.when(p == 0)
    def _():
        process(knew_ref[0], vnew_ref[0], True)

    @pl.when(p > 0)
    def _():
        process(jnp.concatenate([k0_ref[0], k1_ref[0]], axis=1),
                jnp.concatenate([v0_ref[0], v1_ref[0]], axis=1), False)

    @pl.when(p == n_blocks)
    def _():
        lane = lax.broadcasted_iota(jnp.int32, (n_rows, LANES), 1)
        gate = _dot_f32(qrow_ref[0][:, SB_WIDTH:], ks_ref[...])
        past = jnp.logical_and(lane >= 1, lane <= n_blocks)
        sel = _top_blocks(gate, lane, past, False, 1)
        sel = jnp.logical_or(sel > 0.5, lane == 0)
        m_all = m_ref[...]
        m_top = jnp.max(jnp.where(sel, m_all, NEG_BIG), axis=1, keepdims=True)
        coef = jnp.where(sel, jnp.exp(m_all - m_top), 0.0)
        coef = coef / jnp.sum(coef * l_ref[...], axis=1, keepdims=True)
        coef_t = jnp.concatenate([coef, jnp.zeros((LANES - n_rows, LANES), F32)], axis=0).T
        coef_t = coef_t[0:omb_ref.shape[1]]
        col_head = lax.broadcasted_iota(jnp.int32, (LANES, MOBA_WIDTH), 1) // HEAD_DIM
        src_row = lax.broadcasted_iota(jnp.int32, (LANES, MOBA_WIDTH), 0)
        osb = osb_ref[...]
        for tkn in range(n_new):
            spread = (src_row == group_rows + tkn * N_MOBA_HEADS + col_head).astype(BF16)
            weighted = _dot_split_lhs(coef_t, spread, 3) * omb_ref[tkn]
            o_ref[0, tkn:tkn + 1, SB_WIDTH:] = jnp.sum(weighted, axis=0, keepdims=True)
            part = osb[tkn * N_SB_HEADS:(tkn + 1) * N_SB_HEADS] * head_diag
            o_ref[0, tkn:tkn + 1, 0:SB_WIDTH] = jnp.sum(part, axis=0, keepdims=True)


def _sample_attention(q, k_new, v_new, cache_kt, cache_vt, page_table):
    n_seq, n_new, _ = q.shape
    _, _, page = cache_kt.shape
    n_pages = page_table.shape[1]
    assert MOBA_BLOCK == 2 * page and page == LANES and n_pages % 2 == 0 and n_new <= SUBLANES
    assert N_SB_HEADS == N_MOBA_HEADS and 2 * n_new * N_SB_HEADS <= LANES
    n_blocks = n_pages // 2
    steps = n_blocks + 1
    assert steps <= LANES
    steps_pad = -(-steps // BF16_ROWS) * BF16_ROWS
    n_rows = 2 * n_new * N_SB_HEADS

    qh = q.reshape(n_seq, n_new, 2, N_SB_HEADS, HEAD_DIM)
    eye_g = jnp.eye(2, dtype=F32)
    eye_h = jnp.eye(N_SB_HEADS, dtype=F32)
    qrow = jnp.einsum("btghd,gG,hH->bGtHghd", qh, eye_g, eye_h).reshape(n_seq, n_rows, ATT_WIDTH)
    new_t = lambda a: jnp.pad(jnp.swapaxes(a, 1, 2), ((0, 0), (0, 0), (0, page - n_new)))

    def page_map(which):
        def index(b, p, pt):
            blk = n_blocks - jnp.maximum(p, 1)
            return (pt[b * n_pages + 2 * blk + which], 0, 0)
        return index

    seq_map = lambda b, p, pt: (b, 0, 0)
    page_spec = lambda which: pl.BlockSpec((1, ATT_WIDTH, page), page_map(which))
    grid_spec = pltpu.PrefetchScalarGridSpec(
        num_scalar_prefetch=1,
        grid=(n_seq, steps),
        in_specs=[
            pl.BlockSpec((1, n_rows, ATT_WIDTH), seq_map),
            pl.BlockSpec((1, ATT_WIDTH, page), seq_map),
            pl.BlockSpec((1, ATT_WIDTH, page), seq_map),
            page_spec(0), page_spec(1), page_spec(0), page_spec(1),
            pl.BlockSpec((MOBA_BLOCK, MOBA_BLOCK), lambda b, p, pt: (0, 0)),
        ],
        out_specs=pl.BlockSpec((1, n_new, ATT_WIDTH), seq_map),
        scratch_shapes=[
            pltpu.VMEM((n_rows, 1), F32),
            pltpu.VMEM((n_new * N_SB_HEADS, SB_WIDTH), F32),
            pltpu.VMEM((n_rows, LANES), F32),
            pltpu.VMEM((n_rows, LANES), F32),
            pltpu.VMEM((MOBA_WIDTH, LANES), F32),
            pltpu.VMEM((n_new, steps_pad, MOBA_WIDTH), F32),
        ],
    )
    return pl.pallas_call(
        functools.partial(_sample_attn_kernel, n_blocks=n_blocks, n_new=n_new),
        grid_spec=grid_spec,
        out_shape=jax.ShapeDtypeStruct((n_seq, n_new, ATT_WIDTH), F32),
        compiler_params=_cparams(("parallel", "arbitrary")),
        name="sample_attention",
    )(page_table.reshape(-1), qrow, new_t(k_new), new_t(v_new),
      cache_kt, cache_kt, cache_vt, cache_vt, _strict_upper(MOBA_BLOCK))


def _attn_out_kernel(x_ref, osb_ref, omb_ref, wsb_hi_ref, wsb_lo_ref, wmb_hi_ref, wmb_lo_ref, y_ref):
    sb = _dot3(_split(osb_ref[...], 2), (wsb_hi_ref[...], wsb_lo_ref[...]))
    mb = _dot3(_split(omb_ref[...], 2), (wmb_hi_ref[...], wmb_lo_ref[...]))
    y_ref[...] = x_ref[...] + (sb + mb)


def _attn_out(x2d, o_sb, o_mb, w_o_parts):
    n, d = x2d.shape
    tm = _row_tile(n, 512)
    row = lambda i: (i, 0)
    fixed = lambda i: (0, 0)
    return pl.pallas_call(
        _attn_out_kernel,
        grid=(n // tm,),
        in_specs=[
            pl.BlockSpec((tm, d), row),
            pl.BlockSpec((tm, SB_WIDTH), row),
            pl.BlockSpec((tm, MOBA_WIDTH), row),
            pl.BlockSpec((SB_WIDTH, d), fixed),
            pl.BlockSpec((SB_WIDTH, d), fixed),
            pl.BlockSpec((MOBA_WIDTH, d), lambda i: (1, 0)),
            pl.BlockSpec((MOBA_WIDTH, d), lambda i: (1, 0)),
        ],
        out_specs=pl.BlockSpec((tm, d), row),
        out_shape=jax.ShapeDtypeStruct((n, d), F32),
        compiler_params=_cparams(("parallel",)),
        name="attn_out_proj",
    )(x2d, o_sb, o_mb, w_o_parts[0], w_o_parts[1], w_o_parts[0], w_o_parts[1])


def _pool_kernel(*refs, pos0, rows, has_prev):
    if has_prev:
        x_ref, xprev_ref, pre_ref, g_ref, w_ref, sc_ref, y_ref, tail_ref = refs
    else:
        x_ref, pre_ref, g_ref, w_ref, sc_ref, y_ref, tail_ref = refs
    ti = pl.program_id(1)
    g = g_ref[...]
    x = x_ref[0]
    d = x.shape[-1]
    h = _rms(x, g)
    halo = pre_ref[0]
    if has_prev:
        halo = jnp.where(ti == 0, halo, _rms(xprev_ref[0], g))
    pad = (-rows) % SUBLANES
    pieces = [halo, h] + ([jnp.zeros((pad, d), F32)] if pad else [])
    full = jnp.concatenate(pieces, axis=0)
    tail_ref[0] = full[rows:rows + POOL_HALO]

    group = d // len(POOL_WINDOWS)
    pos = pos0 + ti * rows + lax.broadcasted_iota(jnp.int32, (rows, 1), 0)
    s = full
    ys = []
    for gi, win in enumerate(POOL_WINDOWS):
        s = s[:, (group if gi else 0):]
        s = s + pltpu.roll(s, win // 2, 0)
        count = jnp.minimum(win, pos + 1).astype(F32)
        diff = s[POOL_HALO:POOL_HALO + rows, :group] / count - h[:, gi * group:(gi + 1) * group]
        ys.append(_dot(diff.astype(BF16), w_ref[gi]))
    y_ref[0] = x + jnp.concatenate(ys, axis=-1) * sc_ref[...]


def _pool_mixer(x, prefix_h, g, w_pool_bf16, scale, pos0):
    b, t, d = x.shape
    rows = min(t, 512)
    assert t % rows == 0 and rows % POOL_HALO == 0 or t == rows
    has_prev = t > rows
    nt = t // rows
    per = rows // POOL_HALO if has_prev else 1
    in_specs = [pl.BlockSpec((1, rows, d), lambda bi, ti: (bi, ti, 0))]
    args = [x]
    if has_prev:
        in_specs.append(pl.BlockSpec((1, POOL_HALO, d), lambda bi, ti: (bi, jnp.maximum(ti * per - 1, 0), 0)))
        args.append(x)
    fixed2 = lambda bi, ti: (0, 0)
    in_specs += [
        pl.BlockSpec((1, POOL_HALO, d), lambda bi, ti: (bi, 0, 0)),
        pl.BlockSpec((1, d), fixed2),
        pl.BlockSpec(w_pool_bf16.shape, lambda bi, ti: (0, 0, 0)),
        pl.BlockSpec((1, d), fixed2),
    ]
    args += [prefix_h, g.reshape(1, d), w_pool_bf16, scale.reshape(1, d)]
    return pl.pallas_call(
        functools.partial(_pool_kernel, pos0=pos0, rows=rows, has_prev=has_prev),
        grid=(b, nt),
        in_specs=in_specs,
        out_specs=[
            pl.BlockSpec((1, rows, d), lambda bi, ti: (bi, ti, 0)),
            pl.BlockSpec((1, POOL_HALO, d), lambda bi, ti: (bi, 0, 0)),
        ],
        out_shape=[jax.ShapeDtypeStruct((b, t, d), F32), jax.ShapeDtypeStruct((b, POOL_HALO, d), F32)],
        compiler_params=_cparams(("parallel", "arbitrary")),
        name="pool_mixer",
    )(*args)


def _router_kernel(x_ref, g_ref, wr_ref, xn_ref, comb_ref):
    xn = _rms(x_ref[...], g_ref[...])
    xn_ref[...] = xn.astype(BF16)
    logits = _dot_f32(xn, wr_ref[...])
    lane = lax.broadcasted_iota(jnp.int32, logits.shape, 1)
    is_group = jnp.logical_and(lane >= N_EXPERTS, lane < N_EXPERTS + N_EXPERT_GROUPS)
    glog = jnp.where(is_group, logits, -jnp.inf)
    gmax = jnp.max(glog, axis=-1, keepdims=True)
    gsum = jnp.sum(jnp.exp(glog - gmax), axis=-1, keepdims=True)
    g_val = 1.0 / gsum
    g_idx = jnp.min(jnp.where(glog == gmax, lane, 2 * LANES), axis=-1, keepdims=True) - N_EXPERTS
    in_group = jnp.logical_and(lane >= g_idx * EXPERTS_PER_GROUP, lane < (g_idx + 1) * EXPERTS_PER_GROUP)
    elog = jnp.where(in_group, logits, -jnp.inf)
    eexp = jnp.exp(elog - jnp.max(elog, axis=-1, keepdims=True))
    prob = eexp / jnp.sum(eexp, axis=-1, keepdims=True)
    p1 = jnp.max(prob, axis=-1, keepdims=True)
    i1 = jnp.min(jnp.where(jnp.logical_and(in_group, prob == p1), lane, 2 * LANES), axis=-1, keepdims=True)
    rest = jnp.where(jnp.logical_and(in_group, lane != i1), prob, -1.0)
    p2 = jnp.max(rest, axis=-1, keepdims=True)
    i2 = jnp.min(jnp.where(rest == p2, lane, 2 * LANES), axis=-1, keepdims=True)
    norm = g_val / (p1 + p2)
    comb_ref[...] = jnp.where(lane == i1, p1 * norm, 0.0) + jnp.where(lane == i2, p2 * norm, 0.0)


def _router(x2d, g, w_router):
    n, d = x2d.shape
    tm = _row_tile(n, 512)
    row = lambda i: (i, 0)
    fixed = lambda i: (0, 0)
    return pl.pallas_call(
        _router_kernel,
        grid=(n // tm,),
        in_specs=[pl.BlockSpec((tm, d), row), pl.BlockSpec((1, d), fixed), pl.BlockSpec((d, LANES), fixed)],
        out_specs=[pl.BlockSpec((tm, d), row), pl.BlockSpec((tm, LANES), row)],
        out_shape=[jax.ShapeDtypeStruct((n, d), BF16), jax.ShapeDtypeStruct((n, LANES), F32)],
        compiler_params=_cparams(("parallel",)),
        name="ffn_router",
    )(x2d, g.reshape(1, d), w_router)


def _moe_kernel(x_ref, xn_ref, comb_ref, wg_ref, wu_ref, wd_ref, y_ref):
    e = pl.program_id(1)

    @pl.when(e == 0)
    def _():
        y_ref[...] = x_ref[...]

    xn = xn_ref[...]
    gate = _dot(xn, wg_ref[0])
    up = _dot(xn, wu_ref[0])
    ff = gate.shape[-1]
    pick = (lax.broadcasted_iota(jnp.int32, (LANES, ff), 0) == e).astype(BF16)
    weight = _dot_split_lhs(comb_ref[...], pick, 3)
    hidden = gate * jax.nn.sigmoid(gate) * up * weight
    y_ref[...] += _dot(hidden.astype(BF16), wd_ref[0])


def _moe(x2d, xn, comb, wg, wu, wd):
    n, d = x2d.shape
    n_exp, _, ff = wg.shape
    tm = _row_tile(n, 1024)
    row = lambda i, e: (i, 0)
    return pl.pallas_call(
        _moe_kernel,
        grid=(n // tm, n_exp),
        in_specs=[
            pl.BlockSpec((tm, d), row),
            pl.BlockSpec((tm, d), row),
            pl.BlockSpec((tm, LANES), row),
            pl.BlockSpec((1, d, ff), lambda i, e: (e, 0, 0)),
            pl.BlockSpec((1, d, ff), lambda i, e: (e, 0, 0)),
            pl.BlockSpec((1, ff, d), lambda i, e: (e, 0, 0)),
        ],
        out_specs=pl.BlockSpec((tm, d), row),
        out_shape=jax.ShapeDtypeStruct((n, d), F32),
        compiler_params=_cparams(("parallel", "arbitrary")),
        name="moe_experts",
    )(x2d, xn, comb, wg, wu, wd)


def _router_weight(w_group, w_expert):
    d = w_group.shape[0]
    w = jnp.concatenate([w_expert, w_group], axis=1)
    return jnp.pad(w, ((0, 0), (0, LANES - w.shape[1]))).astype(F32).reshape(d, LANES)


def _cache_pages_t(cache):
    layers, pages, page, heads, dim = cache.shape
    return jnp.transpose(cache, (0, 1, 3, 4, 2)).reshape(layers * pages, heads * dim, page)


def _ffn(x2d, g, w_router, wg, wu, wd):
    xn, comb = _router(x2d, g, w_router)
    return _moe(x2d, xn, comb, wg, wu, wd)


def kernel(x_prompt, x_sample, cache_k, cache_v, state_pool, page_table, norm_mix_g, norm_ffn_g, att_w_qkv, att_q_norm_g, att_k_norm_g, att_w_o, pool_w, pool_scale, moe_w_router_group, moe_w_router_expert, moe_w_gate, moe_w_up, moe_w_down):
    batch, seq, d = x_prompt.shape
    n_seq, n_new, _ = x_sample.shape
    depth = norm_mix_g.shape[0]
    page = cache_k.shape[2]
    n_past = page_table.shape[1] * page
    assert depth == 2 and d == ATT_WIDTH and n_past % MOBA_BLOCK == 0

    w_qkv = _split_weights(att_w_qkv[0])
    w_o = _split_weights(att_w_o[0])
    w_pool = pool_w.astype(BF16)
    wg, wu, wd = moe_w_gate.astype(BF16), moe_w_up.astype(BF16), moe_w_down.astype(BF16)
    w_router = [_router_weight(moe_w_router_group[l], moe_w_router_expert[l]) for l in range(depth)]
    ffn = lambda x2d, l: _ffn(x2d, norm_ffn_g[l], w_router[l], wg[l], wu[l], wd[l])

    xp = x_prompt.reshape(batch * seq, d)
    xs = x_sample.reshape(n_seq * n_new, d)

    qg, kg = att_q_norm_g[0].reshape(-1), att_k_norm_g[0].reshape(-1)
    q_p, k_p, v_p = _qkv(xp, norm_mix_g[0], w_qkv, qg, kg)
    o_sb = _sb_attention(q_p, k_p, v_p, batch, seq)
    o_mb = _moba_attention(q_p, k_p, v_p, batch, seq)
    xp = _attn_out(xp, o_sb, o_mb, w_o)

    q_s, k_s, v_s = _qkv(xs, norm_mix_g[0], w_qkv, qg, kg)
    o_s = _sample_attention(
        q_s.reshape(n_seq, n_new, d), k_s.reshape(n_seq, n_new, d), v_s.reshape(n_seq, n_new, d),
        _cache_pages_t(cache_k), _cache_pages_t(cache_v), page_table)
    o_s = o_s.reshape(n_seq * n_new, d)
    xs = _attn_out(xs, o_s[:, :SB_WIDTH], o_s[:, SB_WIDTH:], w_o)

    xp = ffn(xp, 0)
    xs = ffn(xs, 0)

    zeros_p = jnp.zeros((batch, POOL_HALO, d), F32)
    xp3, tail_p = _pool_mixer(xp.reshape(batch, seq, d), zeros_p, norm_mix_g[1], w_pool[0], pool_scale[0], 0)
    prefix_s = jnp.pad(state_pool[0], ((0, 0), (POOL_HALO - POOL_STATE, 0), (0, 0)))
    xs3, tail_s = _pool_mixer(xs.reshape(n_seq, n_new, d), prefix_s, norm_mix_g[1], w_pool[0], pool_scale[0], n_past)
    xp = ffn(xp3.reshape(batch * seq, d), 1)
    xs = ffn(xs3.reshape(n_seq * n_new, d), 1)

    heads = (N_ATT_HEADS, HEAD_DIM)
    return (
        xp.reshape(batch, seq, d),
        xs.reshape(n_seq, n_new, d),
        k_p.reshape(1, batch, seq, *heads),
        v_p.reshape(1, batch, seq, *heads),
        k_s.reshape(1, n_seq, n_new, *heads),
        v_s.reshape(1, n_seq, n_new, *heads),
        tail_p[None, :, POOL_HALO - POOL_STATE:],
        tail_s[None, :, POOL_HALO - POOL_STATE:],
    )
```

```python
import functools

import jax
import jax.numpy as jnp
from jax import lax
from jax.experimental import pallas as pl
from jax.experimental.pallas import tpu as pltpu

F32 = jnp.float32
BF16 = jnp.bfloat16

HEAD_DIM = 64
N_SB_HEADS = 8
N_MOBA_HEADS = 8
N_ATT_HEADS = N_SB_HEADS + N_MOBA_HEADS
ATT_WIDTH = N_ATT_HEADS * HEAD_DIM
SB_WIDTH = N_SB_HEADS * HEAD_DIM
MOBA_WIDTH = N_MOBA_HEADS * HEAD_DIM
ATT_SCALE = HEAD_DIM ** -0.5
MOBA_BLOCK = 256
MOBA_TOPK = 3
MAX_MOBA_BLOCKS = 24
POOL_WINDOWS = (2, 4, 8, 16)
POOL_STATE = max(POOL_WINDOWS) - 1
POOL_HALO = POOL_STATE + 1
N_EXPERT_GROUPS = 4
EXPERTS_PER_GROUP = 8
N_EXPERTS = N_EXPERT_GROUPS * EXPERTS_PER_GROUP
RMS_EPS = 1e-6

LANES = 128
SUBLANES = 8
BF16_ROWS = 16
HEADS_PER_VREG = LANES // HEAD_DIM
SB_TILE = 128
NEG_BIG = -1e30
SB_LOG_CUTOFF = -104.0
VMEM_LIMIT = 48 * 1024 * 1024


def _cparams(sem):
    return pltpu.CompilerParams(dimension_semantics=sem, vmem_limit_bytes=VMEM_LIMIT)


def _dot(a, b):
    return jnp.dot(a, b, preferred_element_type=F32)


def _dot_nt(a, b):
    return lax.dot_general(a, b, (((1,), (1,)), ((), ())), preferred_element_type=F32)


def _split(a, n):
    parts = []
    for _ in range(n - 1):
        p = a.astype(BF16)
        parts.append(p)
        a = a - p.astype(F32)
    parts.append(a.astype(BF16))
    return parts


def _split_weights(w):
    hi = lax.bitcast_convert_type(lax.bitcast_convert_type(w, jnp.uint32) & jnp.uint32(0xFFFF0000), F32)
    return hi.astype(BF16), (w - hi).astype(BF16)


def _dot_split_lhs(a, b_exact, n):
    out = None
    for p in _split(a, n):
        t = _dot(p, b_exact)
        out = t if out is None else out + t
    return out


def _dot_split_rhs(a_exact, b, n):
    out = None
    for p in _split(b, n):
        t = _dot(a_exact, p)
        out = t if out is None else out + t
    return out


def _dot3(a, b, nt=False):
    d = _dot_nt if nt else _dot
    return d(a[0], b[0]) + (d(a[1], b[0]) + d(a[0], b[1]))


def _dot_f32(a, b, nt=False):
    a1, a2, a3 = _split(a, 3)
    b1, b2, b3 = _split(b, 3)
    d = _dot_nt if nt else _dot
    return (d(a1, b1) + (d(a1, b2) + d(a2, b1))) + (d(a1, b3) + d(a3, b1) + d(a2, b2))


def _rms(x, g):
    return x * lax.rsqrt(jnp.mean(x * x, axis=-1, keepdims=True) + RMS_EPS) * g


def _softplus(z):
    return jnp.maximum(z, 0.0) + jnp.log1p(jnp.exp(-jnp.abs(z)))


def _row_tile(n, want):
    t = min(n, want)
    assert n % t == 0, (n, t)
    return t


def _qkv_kernel(x_ref, g_ref, whi_ref, wlo_ref, qg_ref, kg_ref, seg_ref, segt_ref, q_ref, k_ref, v_ref):
    h = _rms(x_ref[...], g_ref[...])
    qkv = _dot3(_split(h, 2), (whi_ref[...], wlo_ref[...]))

    def head_norm(t, gain):
        ssq = _dot_split_lhs(t * t, seg_ref[...], 2)
        inv = lax.rsqrt(ssq * (1.0 / HEAD_DIM) + RMS_EPS)
        return t * _dot_split_lhs(inv, segt_ref[...], 2) * gain

    q = head_norm(qkv[:, :ATT_WIDTH], qg_ref[...])
    k = head_norm(qkv[:, ATT_WIDTH:2 * ATT_WIDTH], kg_ref[...])
    v = qkv[:, 2 * ATT_WIDTH:]
    q_ref[...] = q
    k_ref[...] = k
    v_ref[...] = v


def _qkv(x2d, g, w_parts, qg, kg):
    n, d = x2d.shape
    tm = _row_tile(n, 256)
    head_of_col = jnp.arange(ATT_WIDTH) // HEAD_DIM
    seg = (head_of_col[:, None] == jnp.arange(LANES)[None, :]).astype(BF16)
    segt = seg.T
    row = lambda i: (i, 0)
    fixed = lambda i: (0, 0)
    return pl.pallas_call(
        _qkv_kernel,
        grid=(n // tm,),
        in_specs=[
            pl.BlockSpec((tm, d), row),
            pl.BlockSpec((1, d), fixed),
            pl.BlockSpec((d, 3 * ATT_WIDTH), fixed),
            pl.BlockSpec((d, 3 * ATT_WIDTH), fixed),
            pl.BlockSpec((1, ATT_WIDTH), fixed),
            pl.BlockSpec((1, ATT_WIDTH), fixed),
            pl.BlockSpec((ATT_WIDTH, LANES), fixed),
            pl.BlockSpec((LANES, ATT_WIDTH), fixed),
        ],
        out_specs=[pl.BlockSpec((tm, ATT_WIDTH), row)] * 3,
        out_shape=[jax.ShapeDtypeStruct((n, ATT_WIDTH), F32)] * 3,
        compiler_params=_cparams(("parallel",)),
        name="qkv_proj",
    )(x2d, g.reshape(1, d), w_parts[0], w_parts[1], qg.reshape(1, ATT_WIDTH), kg.reshape(1, ATT_WIDTH), seg, segt)


def _hi_lo(x):
    hi = x.astype(BF16).astype(F32)
    return hi, x - hi


def _swap_heads(x):
    return pltpu.roll(x, HEAD_DIM, 1)


def _stack_queries(q, first):
    hi, lo = _hi_lo(q)
    hs, ls = _swap_heads(hi), _swap_heads(lo)
    head_a = jnp.concatenate([jnp.where(first, hi, ls), jnp.where(first, hi, 0.0)], axis=1)
    head_b = jnp.concatenate([jnp.where(first, hs, lo), jnp.where(first, hs, 0.0)], axis=1)
    return head_a.astype(BF16), head_b.astype(BF16)


def _stack_keys(k, first):
    hi, lo = _hi_lo(k)
    hs, ls = _swap_heads(hi), _swap_heads(lo)
    head_a = jnp.concatenate([jnp.where(first, hi, hs), jnp.where(first, lo, 0.0)], axis=1)
    head_b = jnp.concatenate([jnp.where(first, hs, hi), jnp.where(first, ls, 0.0)], axis=1)
    return head_a.astype(BF16), head_b.astype(BF16)


def _stack_values(v, first):
    hi, lo = _hi_lo(v)
    hs, ls = _swap_heads(hi), _swap_heads(lo)
    return ((jnp.where(first, hi, ls), jnp.where(first, hi, 0.0)),
            (jnp.where(first, hs, lo), jnp.where(first, hs, 0.0)))


def _fold_heads(acc_a, acc_b, first):
    return jnp.where(first, acc_a + _swap_heads(acc_a), acc_b + _swap_heads(acc_b))


def _sb_kernel(q_ref, k_ref, v_ref, u2_ref, o_ref, kcat_ref, vcat_ref, acc_ref, carry_ref):
    t = SB_TILE
    i = pl.program_id(2)
    first = lax.broadcasted_iota(jnp.int32, (t, LANES), 1) < HEAD_DIM

    @pl.when(i == 0)
    def _():
        def build(j, c):
            rows = pl.ds(pl.multiple_of(j * t, t), t)
            for hh, kc in enumerate(_stack_keys(k_ref[rows, :], first)):
                kcat_ref[hh, rows, :] = kc
            for hh, (top, bottom) in enumerate(_stack_values(v_ref[rows, :], first)):
                vcat_ref[hh, j] = jnp.concatenate([top, bottom], axis=0).astype(BF16)
            return c
        lax.fori_loop(0, k_ref.shape[0] // t, build, 0)

    q_heads = _stack_queries(q_ref[...] * ATT_SCALE, first)
    strictly_past = (lax.broadcasted_iota(jnp.int32, (t, t), 1) < lax.broadcasted_iota(jnp.int32, (t, t), 0))
    acc_ref[...] = jnp.zeros_like(acc_ref)
    carry_ref[...] = jnp.zeros_like(carry_ref)

    def block(j, diagonal):
        rows = pl.ds(pl.multiple_of(j * t, t), t)
        for hh in range(HEADS_PER_VREG):
            z = _dot_nt(q_heads[hh], kcat_ref[hh, rows, :])
            sp = _softplus(z)
            log_keep = -sp
            if diagonal:
                log_keep = jnp.where(strictly_past, log_keep, 0.0)
            carry = carry_ref[hh]
            log_between = _dot(jnp.concatenate(_split(log_keep, 2), axis=1), u2_ref[...]) + carry
            w = jnp.exp(z - sp + log_between)
            if diagonal:
                w = jnp.where(strictly_past, w, 0.0)
            acc_ref[hh] += _dot(jnp.concatenate(_split(w, 2), axis=1), vcat_ref[hh, j])
            carry_ref[hh] = carry + jnp.sum(log_keep, axis=-1, keepdims=True)

    def worst():
        return jnp.max(jnp.maximum(carry_ref[0], carry_ref[1]))

    block(i, True)

    def cond(s):
        return jnp.logical_and(s[0] >= 1, s[1] > SB_LOG_CUTOFF)

    def body(s):
        block(s[0], False)
        block(s[0] - 1, False)
        return s[0] - 2, worst()

    j, live = lax.while_loop(cond, body, (i - 1, worst()))

    @pl.when(jnp.logical_and(j == 0, live > SB_LOG_CUTOFF))
    def _():
        block(0, False)

    o_ref[...] = _fold_heads(acc_ref[0], acc_ref[1], first)


def _strict_upper(n):
    r = jnp.arange(n)
    return (r[:, None] > r[None, :]).astype(BF16)


def _sb_attention(q, k, v, batch, seq):
    t = SB_TILE
    assert seq % t == 0
    nq = seq // t
    n_pairs = N_SB_HEADS // HEADS_PER_VREG
    u = _strict_upper(t)
    return pl.pallas_call(
        _sb_kernel,
        grid=(batch, n_pairs, nq),
        in_specs=[
            pl.BlockSpec((t, LANES), lambda b, hp, i: (b * nq + i, hp)),
            pl.BlockSpec((seq, LANES), lambda b, hp, i: (b, hp)),
            pl.BlockSpec((seq, LANES), lambda b, hp, i: (b, hp)),
            pl.BlockSpec((2 * t, t), lambda b, hp, i: (0, 0)),
        ],
        out_specs=pl.BlockSpec((t, LANES), lambda b, hp, i: (b * nq + i, hp)),
        out_shape=jax.ShapeDtypeStruct((batch * seq, SB_WIDTH), F32),
        scratch_shapes=[
            pltpu.VMEM((HEADS_PER_VREG, seq, 2 * LANES), BF16),
            pltpu.VMEM((HEADS_PER_VREG, nq, 2 * t, LANES), BF16),
            pltpu.VMEM((HEADS_PER_VREG, t, LANES), F32),
            pltpu.VMEM((HEADS_PER_VREG, t, 1), F32),
        ],
        compiler_params=_cparams(("parallel", "arbitrary", "arbitrary")),
        name="sb_attention",
    )(q, k, v, jnp.concatenate([u, u], axis=0))


def _top_blocks(gate, pos, allowed, lowest_pos_wins, axis):
    g = jnp.where(allowed, gate, -jnp.inf)
    sel = jnp.zeros(gate.shape, F32)
    for _ in range(MOBA_TOPK):
        m = jnp.max(g, axis=axis, keepdims=True)
        hit = g == m
        if lowest_pos_wins:
            idx = jnp.min(jnp.where(hit, pos, jnp.int32(2 ** 30)), axis=axis, keepdims=True)
        else:
            idx = jnp.max(jnp.where(hit, pos, jnp.int32(-1)), axis=axis, keepdims=True)
        at = pos == idx
        sel = jnp.where(jnp.logical_and(at, m > -jnp.inf), 1.0, sel)
        g = jnp.where(at, -jnp.inf, g)
    return sel


def _moba_kernel(q_ref, k_ref, v_ref, o_ref, kmean_ref, kcat_ref, vcat_ref, m_ref, l_ref, acc_ref, *, n_blocks):
    t = MOBA_BLOCK
    i = pl.program_id(2)
    lane = lax.broadcasted_iota(jnp.int32, (t, LANES), 1)
    first = lane < HEAD_DIM

    @pl.when(i == 0)
    def _():
        kmean_ref[...] = jnp.zeros_like(kmean_ref)

        def build(n, c):
            rows = pl.ds(pl.multiple_of(n * t, t), t)
            k = k_ref[rows, :]
            _put_row(kmean_ref, n, jnp.mean(k, axis=0, keepdims=True))
            for hh, kc in enumerate(_stack_keys(k, first)):
                kcat_ref[hh, rows, :] = kc
            for hh, (hi_lo, _) in enumerate(_stack_values(v_ref[rows, :], first)):
                vcat_ref[hh, rows, :] = hi_lo.astype(BF16)
            return c
        lax.fori_loop(0, n_blocks, build, 0)

    q = q_ref[...]
    q_heads = _stack_queries(q * ATT_SCALE, first)
    q_f32 = (jnp.where(first, q, 0.0), jnp.where(first, 0.0, q))
    kmean = kmean_ref[...]
    bit = jnp.left_shift(1, jnp.minimum(lane, MAX_MOBA_BLOCKS)).astype(F32)
    picked = []
    for hh in range(HEADS_PER_VREG):
        sel = _top_blocks(_dot_f32(q_f32[hh], kmean, nt=True), lane, lane < i, True, 1)
        picked.append(jnp.sum(sel * bit, axis=-1, keepdims=True).astype(jnp.int32))

    def visit(j, keep, n_visit=1):
        rows = pl.ds(pl.multiple_of(j * t, t), n_visit * t)
        for hh in range(HEADS_PER_VREG):
            s = jnp.where(keep[hh], _dot_nt(q_heads[hh], kcat_ref[hh, rows, :]), NEG_BIG)
            m = m_ref[hh]
            m_new = jnp.maximum(m, jnp.max(s, axis=-1, keepdims=True))
            alpha = jnp.exp(m - m_new)
            p = jnp.exp(s - m_new)
            m_ref[hh] = m_new
            l_ref[hh] = alpha * l_ref[hh] + jnp.sum(p, axis=-1, keepdims=True)
            p_hi, p_lo = _split(p, 2)
            vc = vcat_ref[hh, rows, :]
            acc_ref[hh] = alpha * acc_ref[hh] + (_dot(p_hi, vc) + _dot(p_lo, vc))

    m_ref[...] = jnp.full(m_ref.shape, NEG_BIG, F32)
    l_ref[...] = jnp.zeros_like(l_ref)
    acc_ref[...] = jnp.zeros_like(acc_ref)
    causal = (lax.broadcasted_iota(jnp.int32, (t, t), 1) <= lax.broadcasted_iota(jnp.int32, (t, t), 0))
    visit(i, (causal, causal))

    def took(hh, j):
        return jnp.broadcast_to(jnp.bitwise_and(jnp.right_shift(picked[hh], j), 1) > 0, (t, t))

    def body(jj, c):
        j = 2 * jj
        visit(j, [jnp.concatenate([took(hh, j), took(hh, j + 1)], axis=1) for hh in range(HEADS_PER_VREG)], 2)
        return c

    lax.fori_loop(0, i // 2, body, 0)

    @pl.when(i % 2 == 1)
    def _():
        visit(i - 1, [took(hh, i - 1) for hh in range(HEADS_PER_VREG)])
    o_ref[...] = _fold_heads(acc_ref[0] / l_ref[0], acc_ref[1] / l_ref[1], first)


def _moba_attention(q, k, v, batch, seq):
    t = MOBA_BLOCK
    assert seq % t == 0 and seq // t <= MAX_MOBA_BLOCKS
    nq = seq // t
    n_pairs = N_MOBA_HEADS // HEADS_PER_VREG
    first_pair = N_SB_HEADS // HEADS_PER_VREG
    q_map = lambda b, hp, i: (b * nq + i, first_pair + hp)
    kv_map = lambda b, hp, i: (b, first_pair + hp)
    return pl.pallas_call(
        functools.partial(_moba_kernel, n_blocks=nq),
        grid=(batch, n_pairs, nq),
        in_specs=[
            pl.BlockSpec((t, LANES), q_map),
            pl.BlockSpec((seq, LANES), kv_map),
            pl.BlockSpec((seq, LANES), kv_map),
        ],
        out_specs=pl.BlockSpec((t, LANES), lambda b, hp, i: (b * nq + i, hp)),
        out_shape=jax.ShapeDtypeStruct((batch * seq, MOBA_WIDTH), F32),
        scratch_shapes=[
            pltpu.VMEM((LANES, LANES), F32),
            pltpu.VMEM((HEADS_PER_VREG, seq, 2 * LANES), BF16),
            pltpu.VMEM((HEADS_PER_VREG, seq, LANES), BF16),
            pltpu.VMEM((HEADS_PER_VREG, t, 1), F32),
            pltpu.VMEM((HEADS_PER_VREG, t, 1), F32),
            pltpu.VMEM((HEADS_PER_VREG, t, LANES), F32),
        ],
        compiler_params=_cparams(("parallel", "arbitrary", "arbitrary")),
        name="moba_attention",
    )(q, k, v)


def _put_row(ref, p, row):
    base = pl.multiple_of((p // SUBLANES) * SUBLANES, SUBLANES)
    cur = ref[pl.ds(base, SUBLANES), :]
    sub = lax.broadcasted_iota(jnp.int32, cur.shape, 0)
    ref[pl.ds(base, SUBLANES), :] = jnp.where(sub == p % SUBLANES, row, cur)


def _put_col(ref, p, col):
    cur = ref[...]
    lane = lax.broadcasted_iota(jnp.int32, cur.shape, 1)
    ref[...] = jnp.where(lane == p, col, cur)


def _sample_attn_kernel(pt_ref, qrow_ref, knew_ref, vnew_ref, k0_ref, k1_ref, v0_ref, v1_ref, u_ref,
                        o_ref, carry_ref, osb_ref, m_ref, l_ref, ks_ref, omb_ref, *, n_blocks, n_new):
    del pt_ref
    p = pl.program_id(1)
    group_rows = n_new * N_SB_HEADS
    n_rows = 2 * group_rows

    @pl.when(p == 0)
    def _():
        carry_ref[...] = jnp.zeros_like(carry_ref)
        osb_ref[...] = jnp.zeros_like(osb_ref)
        m_ref[...] = jnp.full(m_ref.shape, NEG_BIG, F32)
        l_ref[...] = jnp.zeros_like(l_ref)
        ks_ref[...] = jnp.zeros_like(ks_ref)
        omb_ref[...] = jnp.zeros_like(omb_ref)

    head_diag = (lax.broadcasted_iota(jnp.int32, (N_MOBA_HEADS, MOBA_WIDTH), 1) // HEAD_DIM
                 == lax.broadcasted_iota(jnp.int32, (N_MOBA_HEADS, MOBA_WIDTH), 0)).astype(F32)

    def process(kt, vt, fresh):
        keys = kt.shape[1]
        z = _dot((qrow_ref[0] * ATT_SCALE).astype(BF16), kt.astype(BF16))
        row = lax.broadcasted_iota(jnp.int32, (n_rows, keys), 0)
        sp = _softplus(z)
        log_keep = -sp
        s_mb = z
        if fresh:
            key = lax.broadcasted_iota(jnp.int32, (n_rows, keys), 1)
            tok = (row % group_rows) // N_SB_HEADS
            ok_sb = key < tok
            log_keep = jnp.where(ok_sb, log_keep, 0.0)
            s_mb = jnp.where(key <= tok, z, NEG_BIG)
        u = u_ref[0:keys, 0:keys]
        carry = carry_ref[...]
        log_between = _dot_split_lhs(log_keep, u, 2) + carry
        w_sb = jnp.exp(z - sp + log_between)
        if fresh:
            w_sb = jnp.where(ok_sb, w_sb, 0.0)
        carry_ref[...] = carry + jnp.sum(log_keep, axis=1, keepdims=True)
        m_blk = jnp.max(s_mb, axis=1, keepdims=True)
        e = jnp.exp(s_mb - m_blk)
        _put_col(m_ref, p, m_blk)
        _put_col(l_ref, p, jnp.sum(e, axis=1, keepdims=True))
        w = jnp.where(row < group_rows, w_sb, e).astype(BF16)
        vb = vt.astype(BF16)
        osb_ref[...] += _dot_nt(w[0:group_rows], vb[0:SB_WIDTH])
        o_blk = _dot_nt(w[group_rows:n_rows], vb[SB_WIDTH:])
        for tkn in range(n_new):
            part = o_blk[tkn * N_MOBA_HEADS:(tkn + 1) * N_MOBA_HEADS] * head_diag
            _put_row(omb_ref.at[tkn], p, jnp.sum(part, axis=0, keepdims=True))
        _put_col(ks_ref, p, jnp.sum(kt[SB_WIDTH:], axis=1, keepdims=True))

    @pl.when(p == 0)
    def _():
        process(knew_ref[0], vnew_ref[0], True)

    @pl.when(p > 0)
    def _():
        process(jnp.concatenate([k0_ref[0], k1_ref[0]], axis=1),
                jnp.concatenate([v0_ref[0], v1_ref[0]], axis=1), False)

    @pl.when(p == n_blocks)
    def _():
        lane = lax.broadcasted_iota(jnp.int32, (n_rows, LANES), 1)
        gate = _dot_f32(qrow_ref[0][:, SB_WIDTH:], ks_ref[...])
        past = jnp.logical_and(lane >= 1, lane <= n_blocks)
        sel = _top_blocks(gate, lane, past, False, 1)
        sel = jnp.logical_or(sel > 0.5, lane == 0)
        m_all = m_ref[...]
        m_top = jnp.max(jnp.where(sel, m_all, NEG_BIG), axis=1, keepdims=True)
        coef = jnp.where(sel, jnp.exp(m_all - m_top), 0.0)
        coef = coef / jnp.sum(coef * l_ref[...], axis=1, keepdims=True)
        coef_t = jnp.concatenate([coef, jnp.zeros((LANES - n_rows, LANES), F32)], axis=0).T
        coef_t = coef_t[0:omb_ref.shape[1]]
        col_head = lax.broadcasted_iota(jnp.int32, (LANES, MOBA_WIDTH), 1) // HEAD_DIM
        src_row = lax.broadcasted_iota(jnp.int32, (LANES, MOBA_WIDTH), 0)
        osb = osb_ref[...]
        for tkn in range(n_new):
            spread = (src_row == group_rows + tkn * N_MOBA_HEADS + col_head).astype(BF16)
            weighted = _dot_split_lhs(coef_t, spread, 3) * omb_ref[tkn]
            o_ref[0, tkn:tkn + 1, SB_WIDTH:] = jnp.sum(weighted, axis=0, keepdims=True)
            part = osb[tkn * N_SB_HEADS:(tkn + 1) * N_SB_HEADS] * head_diag
            o_ref[0, tkn:tkn + 1, 0:SB_WIDTH] = jnp.sum(part, axis=0, keepdims=True)


def _sample_attention(q, k_new, v_new, cache_kt, cache_vt, page_table):
    n_seq, n_new, _ = q.shape
    _, _, page = cache_kt.shape
    n_pages = page_table.shape[1]
    assert MOBA_BLOCK == 2 * page and page == LANES and n_pages % 2 == 0 and n_new <= SUBLANES
    assert N_SB_HEADS == N_MOBA_HEADS and 2 * n_new * N_SB_HEADS <= LANES
    n_blocks = n_pages // 2
    steps = n_blocks + 1
    assert steps <= LANES
    steps_pad = -(-steps // BF16_ROWS) * BF16_ROWS
    n_rows = 2 * n_new * N_SB_HEADS

    qh = q.reshape(n_seq, n_new, 2, N_SB_HEADS, HEAD_DIM)
    eye_g = jnp.eye(2, dtype=F32)
    eye_h = jnp.eye(N_SB_HEADS, dtype=F32)
    qrow = jnp.einsum("btghd,gG,hH->bGtHghd", qh, eye_g, eye_h).reshape(n_seq, n_rows, ATT_WIDTH)
    new_t = lambda a: jnp.pad(jnp.swapaxes(a, 1, 2), ((0, 0), (0, 0), (0, page - n_new)))

    def page_map(which):
        def index(b, p, pt):
            blk = n_blocks - jnp.maximum(p, 1)
            return (pt[b * n_pages + 2 * blk + which], 0, 0)
        return index

    seq_map = lambda b, p, pt: (b, 0, 0)
    page_spec = lambda which: pl.BlockSpec((1, ATT_WIDTH, page), page_map(which))
    grid_spec = pltpu.PrefetchScalarGridSpec(
        num_scalar_prefetch=1,
        grid=(n_seq, steps),
        in_specs=[
            pl.BlockSpec((1, n_rows, ATT_WIDTH), seq_map),
            pl.BlockSpec((1, ATT_WIDTH, page), seq_map),
            pl.BlockSpec((1, ATT_WIDTH, page), seq_map),
            page_spec(0), page_spec(1), page_spec(0), page_spec(1),
            pl.BlockSpec((MOBA_BLOCK, MOBA_BLOCK), lambda b, p, pt: (0, 0)),
        ],
        out_specs=pl.BlockSpec((1, n_new, ATT_WIDTH), seq_map),
        scratch_shapes=[
            pltpu.VMEM((n_rows, 1), F32),
            pltpu.VMEM((n_new * N_SB_HEADS, SB_WIDTH), F32),
            pltpu.VMEM((n_rows, LANES), F32),
            pltpu.VMEM((n_rows, LANES), F32),
            pltpu.VMEM((MOBA_WIDTH, LANES), F32),
            pltpu.VMEM((n_new, steps_pad, MOBA_WIDTH), F32),
        ],
    )
    return pl.pallas_call(
        functools.partial(_sample_attn_kernel, n_blocks=n_blocks, n_new=n_new),
        grid_spec=grid_spec,
        out_shape=jax.ShapeDtypeStruct((n_seq, n_new, ATT_WIDTH), F32),
        compiler_params=_cparams(("parallel", "arbitrary")),
        name="sample_attention",
    )(page_table.reshape(-1), qrow, new_t(k_new), new_t(v_new),
      cache_kt, cache_kt, cache_vt, cache_vt, _strict_upper(MOBA_BLOCK))


def _attn_out_kernel(x_ref, osb_ref, omb_ref, wsb_hi_ref, wsb_lo_ref, wmb_hi_ref, wmb_lo_ref, y_ref):
    sb = _dot3(_split(osb_ref[...], 2), (wsb_hi_ref[...], wsb_lo_ref[...]))
    mb = _dot3(_split(omb_ref[...], 2), (wmb_hi_ref[...], wmb_lo_ref[...]))
    y_ref[...] = x_ref[...] + (sb + mb)


def _attn_out(x2d, o_sb, o_mb, w_o_parts):
    n, d = x2d.shape
    tm = _row_tile(n, 512)
    row = lambda i: (i, 0)
    fixed = lambda i: (0, 0)
    return pl.pallas_call(
        _attn_out_kernel,
        grid=(n // tm,),
        in_specs=[
            pl.BlockSpec((tm, d), row),
            pl.BlockSpec((tm, SB_WIDTH), row),
            pl.BlockSpec((tm, MOBA_WIDTH), row),
            pl.BlockSpec((SB_WIDTH, d), fixed),
            pl.BlockSpec((SB_WIDTH, d), fixed),
            pl.BlockSpec((MOBA_WIDTH, d), lambda i: (1, 0)),
            pl.BlockSpec((MOBA_WIDTH, d), lambda i: (1, 0)),
        ],
        out_specs=pl.BlockSpec((tm, d), row),
        out_shape=jax.ShapeDtypeStruct((n, d), F32),
        compiler_params=_cparams(("parallel",)),
        name="attn_out_proj",
    )(x2d, o_sb, o_mb, w_o_parts[0], w_o_parts[1], w_o_parts[0], w_o_parts[1])


def _pool_kernel(*refs, pos0, rows, has_prev):
    if has_prev:
        x_ref, xprev_ref, pre_ref, g_ref, w_ref, sc_ref, y_ref, tail_ref = refs
    else:
        x_ref, pre_ref, g_ref, w_ref, sc_ref, y_ref, tail_ref = refs
    ti = pl.program_id(1)
    g = g_ref[...]
    x = x_ref[0]
    d = x.shape[-1]
    h = _rms(x, g)
    halo = pre_ref[0]
    if has_prev:
        halo = jnp.where(ti == 0, halo, _rms(xprev_ref[0], g))
    pad = (-rows) % SUBLANES
    pieces = [halo, h] + ([jnp.zeros((pad, d), F32)] if pad else [])
    full = jnp.concatenate(pieces, axis=0)
    tail_ref[0] = full[rows:rows + POOL_HALO]

    group = d // len(POOL_WINDOWS)
    pos = pos0 + ti * rows + lax.broadcasted_iota(jnp.int32, (rows, 1), 0)
    s = full
    ys = []
    for gi, win in enumerate(POOL_WINDOWS):
        s = s[:, (group if gi else 0):]
        s = s + pltpu.roll(s, win // 2, 0)
        count = jnp.minimum(win, pos + 1).astype(F32)
        diff = s[POOL_HALO:POOL_HALO + rows, :group] / count - h[:, gi * group:(gi + 1) * group]
        ys.append(_dot(diff.astype(BF16), w_ref[gi]))
    y_ref[0] = x + jnp.concatenate(ys, axis=-1) * sc_ref[...]


def _pool_mixer(x, prefix_h, g, w_pool_bf16, scale, pos0):
    b, t, d = x.shape
    rows = min(t, 512)
    assert t % rows == 0 and rows % POOL_HALO == 0 or t == rows
    has_prev = t > rows
    nt = t // rows
    per = rows // POOL_HALO if has_prev else 1
    in_specs = [pl.BlockSpec((1, rows, d), lambda bi, ti: (bi, ti, 0))]
    args = [x]
    if has_prev:
        in_specs.append(pl.BlockSpec((1, POOL_HALO, d), lambda bi, ti: (bi, jnp.maximum(ti * per - 1, 0), 0)))
        args.append(x)
    fixed2 = lambda bi, ti: (0, 0)
    in_specs += [
        pl.BlockSpec((1, POOL_HALO, d), lambda bi, ti: (bi, 0, 0)),
        pl.BlockSpec((1, d), fixed2),
        pl.BlockSpec(w_pool_bf16.shape, lambda bi, ti: (0, 0, 0)),
        pl.BlockSpec((1, d), fixed2),
    ]
    args += [prefix_h, g.reshape(1, d), w_pool_bf16, scale.reshape(1, d)]
    return pl.pallas_call(
        functools.partial(_pool_kernel, pos0=pos0, rows=rows, has_prev=has_prev),
        grid=(b, nt),
        in_specs=in_specs,
        out_specs=[
            pl.BlockSpec((1, rows, d), lambda bi, ti: (bi, ti, 0)),
            pl.BlockSpec((1, POOL_HALO, d), lambda bi, ti: (bi, 0, 0)),
        ],
        out_shape=[jax.ShapeDtypeStruct((b, t, d), F32), jax.ShapeDtypeStruct((b, POOL_HALO, d), F32)],
        compiler_params=_cparams(("parallel", "arbitrary")),
        name="pool_mixer",
    )(*args)


def _router_kernel(x_ref, g_ref, wr_ref, xn_ref, comb_ref, route_ref):
    xn = _rms(x_ref[...], g_ref[...])
    xn_ref[...] = xn.astype(xn_ref.dtype)
    logits = _dot_f32(xn, wr_ref[...])
    lane = lax.broadcasted_iota(jnp.int32, logits.shape, 1)
    is_group = jnp.logical_and(lane >= N_EXPERTS, lane < N_EXPERTS + N_EXPERT_GROUPS)
    glog = jnp.where(is_group, logits, -jnp.inf)
    gmax = jnp.max(glog, axis=-1, keepdims=True)
    gsum = jnp.sum(jnp.exp(glog - gmax), axis=-1, keepdims=True)
    g_val = 1.0 / gsum
    g_idx = jnp.min(jnp.where(glog == gmax, lane, 2 * LANES), axis=-1, keepdims=True) - N_EXPERTS
    in_group = jnp.logical_and(lane >= g_idx * EXPERTS_PER_GROUP, lane < (g_idx + 1) * EXPERTS_PER_GROUP)
    elog = jnp.where(in_group, logits, -jnp.inf)
    eexp = jnp.exp(elog - jnp.max(elog, axis=-1, keepdims=True))
    prob = eexp / jnp.sum(eexp, axis=-1, keepdims=True)
    p1 = jnp.max(prob, axis=-1, keepdims=True)
    i1 = jnp.min(jnp.where(jnp.logical_and(in_group, prob == p1), lane, 2 * LANES), axis=-1, keepdims=True)
    rest = jnp.where(jnp.logical_and(in_group, lane != i1), prob, -1.0)
    p2 = jnp.max(rest, axis=-1, keepdims=True)
    i2 = jnp.min(jnp.where(rest == p2, lane, 2 * LANES), axis=-1, keepdims=True)
    norm = g_val / (p1 + p2)
    comb_ref[...] = jnp.where(lane == i1, p1 * norm, 0.0) + jnp.where(lane == i2, p2 * norm, 0.0)
    route_ref[...] = (jnp.where(lane == 0, p1 * norm, 0.0) + jnp.where(lane == 1, p2 * norm, 0.0)
                      + jnp.where(lane == 2, i1.astype(F32), 0.0) + jnp.where(lane == 3, i2.astype(F32), 0.0))


def _router(x2d, g, w_router, xn_dtype):
    n, d = x2d.shape
    tm = _row_tile(n, 512)
    row = lambda i: (i, 0)
    fixed = lambda i: (0, 0)
    return pl.pallas_call(
        _router_kernel,
        grid=(n // tm,),
        in_specs=[pl.BlockSpec((tm, d), row), pl.BlockSpec((1, d), fixed), pl.BlockSpec((d, LANES), fixed)],
        out_specs=[pl.BlockSpec((tm, d), row), pl.BlockSpec((tm, LANES), row), pl.BlockSpec((tm, LANES), row)],
        out_shape=[jax.ShapeDtypeStruct((n, d), xn_dtype), jax.ShapeDtypeStruct((n, LANES), F32),
                   jax.ShapeDtypeStruct((n, LANES), F32)],
        compiler_params=_cparams(("parallel",)),
        name="ffn_router",
    )(x2d, g.reshape(1, d), w_router)


MOE_TILE = 256
GATHER_UNROLL = 8


def _dispatch_plan(route, n_exp, tm):
    n = route.shape[0]
    ids = route[:, 2:4].astype(jnp.int32)
    e = jnp.concatenate([ids[:, 0], ids[:, 1]])
    onehot = (e[:, None] == jnp.arange(n_exp, dtype=jnp.int32)[None, :]).astype(jnp.int32)
    seen = jnp.cumsum(onehot, axis=0)
    counts = seen[-1]
    rank = jnp.sum(onehot * seen, axis=1) - 1
    padded = ((counts + tm - 1) // tm) * tm
    group_start = jnp.cumsum(padded) - padded
    dest = group_start[e] + rank
    n_rows = 2 * n + n_exp * tm
    tok = jnp.concatenate([jnp.arange(n, dtype=jnp.int32)] * 2)
    row_token = jnp.zeros((n_rows,), jnp.int32).at[dest].set(tok, unique_indices=True)
    n_tiles = n_rows // tm
    used_tiles = (jnp.sum(padded) // tm).astype(jnp.int32)
    tile_start = jnp.arange(n_tiles, dtype=jnp.int32) * tm
    group_end = group_start + padded
    tile_expert = jnp.sum((tile_start[:, None] >= group_end[None, :]).astype(jnp.int32), axis=1)
    last = tile_expert[jnp.maximum(used_tiles - 1, 0)]
    tile_expert = jnp.where(jnp.arange(n_tiles) < used_tiles, tile_expert, last).astype(jnp.int32)
    return row_token, tile_expert, used_tiles.reshape(1), dest[:n], dest[n:]


def _expert_rows_kernel(te_ref, rt_ref, used_ref, x_hbm, wg_ref, wu_ref, wd_ref, y_ref, xbuf, sem):
    del te_ref
    tm = y_ref.shape[0]
    i = pl.program_id(0)
    used = used_ref[0]
    slot = i % 2

    def row_copy(tile, r, slot_):
        tok = rt_ref[tile * tm + r]
        return pltpu.make_async_copy(x_hbm.at[pl.ds(tok, 1), :], xbuf.at[slot_, pl.ds(r, 1), :], sem.at[slot_])

    def start_gather(tile, slot_):
        def trip(c, carry):
            for k in range(GATHER_UNROLL):
                row_copy(tile, c * GATHER_UNROLL + k, slot_).start()
            return carry
        lax.fori_loop(0, tm // GATHER_UNROLL, trip, 0)

    @pl.when(i == 0)
    def _():
        start_gather(0, 0)

    @pl.when(i + 1 < used)
    def _():
        start_gather(i + 1, 1 - slot)

    @pl.when(i < used)
    def _():
        pltpu.make_async_copy(xbuf.at[slot], xbuf.at[slot], sem.at[slot]).wait()
        x = xbuf[slot].astype(BF16)
        gate = _dot(x, wg_ref[0])
        up = _dot(x, wu_ref[0])
        hidden = gate * jax.nn.sigmoid(gate) * up
        y_ref[...] = _dot(hidden.astype(BF16), wd_ref[0])

    @pl.when(i >= used)
    def _():
        y_ref[...] = jnp.zeros_like(y_ref)


def _expert_rows(xn, row_token, tile_expert, used_tiles, wg, wu, wd):
    n, d = xn.shape
    n_exp, _, ff = wg.shape
    tm = MOE_TILE
    n_rows = row_token.shape[0]
    w_map = lambda i, te, rt, used: (te[i], 0, 0)
    grid_spec = pltpu.PrefetchScalarGridSpec(
        num_scalar_prefetch=3,
        grid=(n_rows // tm,),
        in_specs=[
            pl.BlockSpec(memory_space=pl.ANY),
            pl.BlockSpec((1, d, ff), w_map),
            pl.BlockSpec((1, d, ff), w_map),
            pl.BlockSpec((1, ff, d), w_map),
        ],
        out_specs=pl.BlockSpec((tm, d), lambda i, te, rt, used: (i, 0)),
        scratch_shapes=[pltpu.VMEM((2, tm, d), F32), pltpu.SemaphoreType.DMA((2,))],
    )
    return pl.pallas_call(
        _expert_rows_kernel,
        grid_spec=grid_spec,
        out_shape=jax.ShapeDtypeStruct((n_rows, d), F32),
        compiler_params=_cparams(("arbitrary",)),
        name="moe_expert_rows",
    )(tile_expert, row_token, used_tiles, xn, wg, wu, wd)


def _moe_mix_kernel(p1_ref, p2_ref, x_ref, route_ref, ys_hbm, o_ref, buf, sem):
    tc = o_ref.shape[0]
    i = pl.program_id(0)
    n = pl.num_programs(0)
    slot = i % 2

    def start_gather(tile, slot_):
        def trip(c, carry):
            for k in range(GATHER_UNROLL):
                r = c * GATHER_UNROLL + k
                for which, pos_ref in enumerate((p1_ref, p2_ref)):
                    pltpu.make_async_copy(ys_hbm.at[pl.ds(pos_ref[tile * tc + r], 1), :],
                                          buf.at[slot_, which, pl.ds(r, 1), :], sem.at[slot_]).start()
            return carry
        lax.fori_loop(0, tc // GATHER_UNROLL, trip, 0)

    @pl.when(i == 0)
    def _():
        start_gather(0, 0)

    @pl.when(i + 1 < n)
    def _():
        start_gather(i + 1, 1 - slot)

    pltpu.make_async_copy(buf.at[slot], buf.at[slot], sem.at[slot]).wait()
    route = route_ref[...]
    o_ref[...] = x_ref[...] + (route[:, 0:1] * buf[slot, 0] + route[:, 1:2] * buf[slot, 1])


def _moe_mix(x2d, route, ys, pos1, pos2):
    n, d = x2d.shape
    tc = MOE_TILE
    row = lambda i, p1, p2: (i, 0)
    grid_spec = pltpu.PrefetchScalarGridSpec(
        num_scalar_prefetch=2,
        grid=(n // tc,),
        in_specs=[pl.BlockSpec((tc, d), row), pl.BlockSpec((tc, LANES), row), pl.BlockSpec(memory_space=pl.ANY)],
        out_specs=pl.BlockSpec((tc, d), row),
        scratch_shapes=[pltpu.VMEM((2, 2, tc, d), F32), pltpu.SemaphoreType.DMA((2,))],
    )
    return pl.pallas_call(
        _moe_mix_kernel,
        grid_spec=grid_spec,
        out_shape=jax.ShapeDtypeStruct((n, d), F32),
        compiler_params=_cparams(("arbitrary",)),
        name="moe_mix",
    )(pos1, pos2, x2d, route, ys)


def _moe_kernel(x_ref, xn_ref, comb_ref, wg_ref, wu_ref, wd_ref, y_ref):
    e = pl.program_id(1)

    @pl.when(e == 0)
    def _():
        y_ref[...] = x_ref[...]

    xn = xn_ref[...]
    gate = _dot(xn, wg_ref[0])
    up = _dot(xn, wu_ref[0])
    ff = gate.shape[-1]
    pick = (lax.broadcasted_iota(jnp.int32, (LANES, ff), 0) == e).astype(BF16)
    weight = _dot_split_lhs(comb_ref[...], pick, 3)
    hidden = gate * jax.nn.sigmoid(gate) * up * weight
    y_ref[...] += _dot(hidden.astype(BF16), wd_ref[0])


def _moe(x2d, xn, comb, wg, wu, wd):
    n, d = x2d.shape
    n_exp, _, ff = wg.shape
    tm = _row_tile(n, 1024)
    row = lambda i, e: (i, 0)
    return pl.pallas_call(
        _moe_kernel,
        grid=(n // tm, n_exp),
        in_specs=[
            pl.BlockSpec((tm, d), row),
            pl.BlockSpec((tm, d), row),
            pl.BlockSpec((tm, LANES), row),
            pl.BlockSpec((1, d, ff), lambda i, e: (e, 0, 0)),
            pl.BlockSpec((1, d, ff), lambda i, e: (e, 0, 0)),
            pl.BlockSpec((1, ff, d), lambda i, e: (e, 0, 0)),
        ],
        out_specs=pl.BlockSpec((tm, d), row),
        out_shape=jax.ShapeDtypeStruct((n, d), F32),
        compiler_params=_cparams(("parallel", "arbitrary")),
        name="moe_experts",
    )(x2d, xn, comb, wg, wu, wd)


def _router_weight(w_group, w_expert):
    d = w_group.shape[0]
    w = jnp.concatenate([w_expert, w_group], axis=1)
    return jnp.pad(w, ((0, 0), (0, LANES - w.shape[1]))).astype(F32).reshape(d, LANES)


def _cache_pages_t(cache):
    layers, pages, page, heads, dim = cache.shape
    return jnp.transpose(cache, (0, 1, 3, 4, 2)).reshape(layers * pages, heads * dim, page)


def _ffn(x2d, g, w_router, wg, wu, wd):
    n = x2d.shape[0]
    n_exp = wg.shape[0]
    grouped_rows = 2 * n + n_exp * MOE_TILE
    if n % MOE_TILE == 0 and n * n_exp > 4 * grouped_rows:
        xn, _, route = _router(x2d, g, w_router, F32)
        row_token, tile_expert, used_tiles, pos1, pos2 = _dispatch_plan(route, n_exp, MOE_TILE)
        ys = _expert_rows(xn, row_token, tile_expert, used_tiles, wg, wu, wd)
        return _moe_mix(x2d, route, ys, pos1, pos2)
    xn, comb, _ = _router(x2d, g, w_router, BF16)
    return _moe(x2d, xn, comb, wg, wu, wd)


def kernel(x_prompt, x_sample, cache_k, cache_v, state_pool, page_table, norm_mix_g, norm_ffn_g, att_w_qkv, att_q_norm_g, att_k_norm_g, att_w_o, pool_w, pool_scale, moe_w_router_group, moe_w_router_expert, moe_w_gate, moe_w_up, moe_w_down):
    batch, seq, d = x_prompt.shape
    n_seq, n_new, _ = x_sample.shape
    depth = norm_mix_g.shape[0]
    page = cache_k.shape[2]
    n_past = page_table.shape[1] * page
    assert depth == 2 and d == ATT_WIDTH and n_past % MOBA_BLOCK == 0

    w_qkv = _split_weights(att_w_qkv[0])
    w_o = _split_weights(att_w_o[0])
    w_pool = pool_w.astype(BF16)
    wg, wu, wd = moe_w_gate.astype(BF16), moe_w_up.astype(BF16), moe_w_down.astype(BF16)
    w_router = [_router_weight(moe_w_router_group[l], moe_w_router_expert[l]) for l in range(depth)]
    ffn = lambda x2d, l: _ffn(x2d, norm_ffn_g[l], w_router[l], wg[l], wu[l], wd[l])

    xp = x_prompt.reshape(batch * seq, d)
    xs = x_sample.reshape(n_seq * n_new, d)

    qg, kg = att_q_norm_g[0].reshape(-1), att_k_norm_g[0].reshape(-1)
    q_p, k_p, v_p = _qkv(xp, norm_mix_g[0], w_qkv, qg, kg)
    o_sb = _sb_attention(q_p, k_p, v_p, batch, seq)
    o_mb = _moba_attention(q_p, k_p, v_p, batch, seq)
    xp = _attn_out(xp, o_sb, o_mb, w_o)

    q_s, k_s, v_s = _qkv(xs, norm_mix_g[0], w_qkv, qg, kg)
    o_s = _sample_attention(
        q_s.reshape(n_seq, n_new, d), k_s.reshape(n_seq, n_new, d), v_s.reshape(n_seq, n_new, d),
        _cache_pages_t(cache_k), _cache_pages_t(cache_v), page_table)
    o_s = o_s.reshape(n_seq * n_new, d)
    xs = _attn_out(xs, o_s[:, :SB_WIDTH], o_s[:, SB_WIDTH:], w_o)

    xp = ffn(xp, 0)
    xs = ffn(xs, 0)

    zeros_p = jnp.zeros((batch, POOL_HALO, d), F32)
    xp3, tail_p = _pool_mixer(xp.reshape(batch, seq, d), zeros_p, norm_mix_g[1], w_pool[0], pool_scale[0], 0)
    prefix_s = jnp.pad(state_pool[0], ((0, 0), (POOL_HALO - POOL_STATE, 0), (0, 0)))
    xs3, tail_s = _pool_mixer(xs.reshape(n_seq, n_new, d), prefix_s, norm_mix_g[1], w_pool[0], pool_scale[0], n_past)
    xp = ffn(xp3.reshape(batch * seq, d), 1)
    xs = ffn(xs3.reshape(n_seq * n_new, d), 1)

    heads = (N_ATT_HEADS, HEAD_DIM)
    return (
        xp.reshape(batch, seq, d),
        xs.reshape(n_seq, n_new, d),
        k_p.reshape(1, batch, seq, *heads),
        v_p.reshape(1, batch, seq, *heads),
        k_s.reshape(1, n_seq, n_new, *heads),
        v_s.reshape(1, n_seq, n_new, *heads),
        tail_p[None, :, POOL_HALO - POOL_STATE:],
        tail_s[None, :, POOL_HALO - POOL_STATE:],
    )
```

```python
import functools

import jax
import jax.numpy as jnp
from jax import lax
from jax.experimental import pallas as pl
from jax.experimental.pallas import tpu as pltpu

F32 = jnp.float32
BF16 = jnp.bfloat16

HEAD_DIM = 64
N_SB_HEADS = 8
N_MOBA_HEADS = 8
N_ATT_HEADS = N_SB_HEADS + N_MOBA_HEADS
ATT_WIDTH = N_ATT_HEADS * HEAD_DIM
SB_WIDTH = N_SB_HEADS * HEAD_DIM
MOBA_WIDTH = N_MOBA_HEADS * HEAD_DIM
ATT_SCALE = HEAD_DIM ** -0.5
MOBA_BLOCK = 256
MOBA_TOPK = 3
MAX_MOBA_BLOCKS = 24
POOL_WINDOWS = (2, 4, 8, 16)
POOL_STATE = max(POOL_WINDOWS) - 1
POOL_HALO = POOL_STATE + 1
N_EXPERT_GROUPS = 4
EXPERTS_PER_GROUP = 8
N_EXPERTS = N_EXPERT_GROUPS * EXPERTS_PER_GROUP
RMS_EPS = 1e-6

LANES = 128
SUBLANES = 8
BF16_ROWS = 16
HEADS_PER_VREG = LANES // HEAD_DIM
SB_TILE = 128
NEG_BIG = -1e30
SB_LOG_CUTOFF = -104.0
VMEM_LIMIT = 48 * 1024 * 1024


def _cparams(sem):
    return pltpu.CompilerParams(dimension_semantics=sem, vmem_limit_bytes=VMEM_LIMIT)


def _dot(a, b):
    return jnp.dot(a, b, preferred_element_type=F32)


def _dot_nt(a, b):
    return lax.dot_general(a, b, (((1,), (1,)), ((), ())), preferred_element_type=F32)


def _split(a, n):
    parts = []
    for _ in range(n - 1):
        p = a.astype(BF16)
        parts.append(p)
        a = a - p.astype(F32)
    parts.append(a.astype(BF16))
    return parts


def _split_weights(w):
    hi = lax.bitcast_convert_type(lax.bitcast_convert_type(w, jnp.uint32) & jnp.uint32(0xFFFF0000), F32)
    return hi.astype(BF16), (w - hi).astype(BF16)


def _dot_split_lhs(a, b_exact, n):
    out = None
    for p in _split(a, n):
        t = _dot(p, b_exact)
        out = t if out is None else out + t
    return out


def _dot_split_rhs(a_exact, b, n):
    out = None
    for p in _split(b, n):
        t = _dot(a_exact, p)
        out = t if out is None else out + t
    return out


def _dot3(a, b, nt=False):
    d = _dot_nt if nt else _dot
    return d(a[0], b[0]) + (d(a[1], b[0]) + d(a[0], b[1]))


def _dot_f32(a, b, nt=False):
    a1, a2, a3 = _split(a, 3)
    b1, b2, b3 = _split(b, 3)
    d = _dot_nt if nt else _dot
    return (d(a1, b1) + (d(a1, b2) + d(a2, b1))) + (d(a1, b3) + d(a3, b1) + d(a2, b2))


def _rms(x, g):
    return x * lax.rsqrt(jnp.mean(x * x, axis=-1, keepdims=True) + RMS_EPS) * g


def _softplus(z):
    return jnp.maximum(z, 0.0) + jnp.log1p(jnp.exp(-jnp.abs(z)))


def _row_tile(n, want):
    t = min(n, want)
    assert n % t == 0, (n, t)
    return t


def _qkv_kernel(x_ref, g_ref, whi_ref, wlo_ref, qg_ref, kg_ref, seg_ref, segt_ref, q_ref, k_ref, v_ref):
    h = _rms(x_ref[...], g_ref[...])
    qkv = _dot3(_split(h, 2), (whi_ref[...], wlo_ref[...]))

    def head_norm(t, gain):
        ssq = _dot_split_lhs(t * t, seg_ref[...], 2)
        inv = lax.rsqrt(ssq * (1.0 / HEAD_DIM) + RMS_EPS)
        return t * _dot_split_lhs(inv, segt_ref[...], 2) * gain

    q = head_norm(qkv[:, :ATT_WIDTH], qg_ref[...])
    k = head_norm(qkv[:, ATT_WIDTH:2 * ATT_WIDTH], kg_ref[...])
    v = qkv[:, 2 * ATT_WIDTH:]
    q_ref[...] = q
    k_ref[...] = k
    v_ref[...] = v


def _qkv(x2d, g, w_parts, qg, kg):
    n, d = x2d.shape
    tm = _row_tile(n, 256)
    head_of_col = jnp.arange(ATT_WIDTH) // HEAD_DIM
    seg = (head_of_col[:, None] == jnp.arange(LANES)[None, :]).astype(BF16)
    segt = seg.T
    row = lambda i: (i, 0)
    fixed = lambda i: (0, 0)
    return pl.pallas_call(
        _qkv_kernel,
        grid=(n // tm,),
        in_specs=[
            pl.BlockSpec((tm, d), row),
            pl.BlockSpec((1, d), fixed),
            pl.BlockSpec((d, 3 * ATT_WIDTH), fixed),
            pl.BlockSpec((d, 3 * ATT_WIDTH), fixed),
            pl.BlockSpec((1, ATT_WIDTH), fixed),
            pl.BlockSpec((1, ATT_WIDTH), fixed),
            pl.BlockSpec((ATT_WIDTH, LANES), fixed),
            pl.BlockSpec((LANES, ATT_WIDTH), fixed),
        ],
        out_specs=[pl.BlockSpec((tm, ATT_WIDTH), row)] * 3,
        out_shape=[jax.ShapeDtypeStruct((n, ATT_WIDTH), F32)] * 3,
        compiler_params=_cparams(("parallel",)),
        name="qkv_proj",
    )(x2d, g.reshape(1, d), w_parts[0], w_parts[1], qg.reshape(1, ATT_WIDTH), kg.reshape(1, ATT_WIDTH), seg, segt)


def _hi_lo(x):
    hi = x.astype(BF16).astype(F32)
    return hi, x - hi


def _swap_heads(x):
    return pltpu.roll(x, HEAD_DIM, 1)


def _stack_queries(q, first):
    hi, lo = _hi_lo(q)
    hs, ls = _swap_heads(hi), _swap_heads(lo)
    head_a = jnp.concatenate([jnp.where(first, hi, ls), jnp.where(first, hi, 0.0)], axis=1)
    head_b = jnp.concatenate([jnp.where(first, hs, lo), jnp.where(first, hs, 0.0)], axis=1)
    return head_a.astype(BF16), head_b.astype(BF16)


def _stack_keys(k, first):
    hi, lo = _hi_lo(k)
    hs, ls = _swap_heads(hi), _swap_heads(lo)
    head_a = jnp.concatenate([jnp.where(first, hi, hs), jnp.where(first, lo, 0.0)], axis=1)
    head_b = jnp.concatenate([jnp.where(first, hs, hi), jnp.where(first, ls, 0.0)], axis=1)
    return head_a.astype(BF16), head_b.astype(BF16)


def _stack_values(v, first):
    hi, lo = _hi_lo(v)
    hs, ls = _swap_heads(hi), _swap_heads(lo)
    return ((jnp.where(first, hi, ls), jnp.where(first, hi, 0.0)),
            (jnp.where(first, hs, lo), jnp.where(first, hs, 0.0)))


def _fold_heads(acc_a, acc_b, first):
    return jnp.where(first, acc_a + _swap_heads(acc_a), acc_b + _swap_heads(acc_b))


def _sb_kernel(q_ref, k_ref, v_ref, u2_ref, o_ref, kcat_ref, vcat_ref, acc_ref, carry_ref):
    t = SB_TILE
    i = pl.program_id(2)
    first = lax.broadcasted_iota(jnp.int32, (t, LANES), 1) < HEAD_DIM

    @pl.when(i == 0)
    def _():
        def build(j, c):
            rows = pl.ds(pl.multiple_of(j * t, t), t)
            for hh, kc in enumerate(_stack_keys(k_ref[rows, :], first)):
                kcat_ref[hh, rows, :] = kc
            for hh, (top, bottom) in enumerate(_stack_values(v_ref[rows, :], first)):
                vcat_ref[hh, j] = jnp.concatenate([top, bottom], axis=0).astype(BF16)
            return c
        lax.fori_loop(0, k_ref.shape[0] // t, build, 0)

    q_heads = _stack_queries(q_ref[...] * ATT_SCALE, first)
    strictly_past = (lax.broadcasted_iota(jnp.int32, (t, t), 1) < lax.broadcasted_iota(jnp.int32, (t, t), 0))
    acc_ref[...] = jnp.zeros_like(acc_ref)
    carry_ref[...] = jnp.zeros_like(carry_ref)

    def block(j, diagonal):
        rows = pl.ds(pl.multiple_of(j * t, t), t)
        for hh in range(HEADS_PER_VREG):
            z = _dot_nt(q_heads[hh], kcat_ref[hh, rows, :])
            sp = _softplus(z)
            log_keep = -sp
            if diagonal:
                log_keep = jnp.where(strictly_past, log_keep, 0.0)
            carry = carry_ref[hh]
            log_between = _dot(jnp.concatenate(_split(log_keep, 2), axis=1), u2_ref[...]) + carry
            w = jnp.exp(z - sp + log_between)
            if diagonal:
                w = jnp.where(strictly_past, w, 0.0)
            acc_ref[hh] += _dot(jnp.concatenate(_split(w, 2), axis=1), vcat_ref[hh, j])
            carry_ref[hh] = carry + jnp.sum(log_keep, axis=-1, keepdims=True)

    def worst():
        return jnp.max(jnp.maximum(carry_ref[0], carry_ref[1]))

    block(i, True)

    def cond(s):
        return jnp.logical_and(s[0] >= 1, s[1] > SB_LOG_CUTOFF)

    def body(s):
        block(s[0], False)
        block(s[0] - 1, False)
        return s[0] - 2, worst()

    j, live = lax.while_loop(cond, body, (i - 1, worst()))

    @pl.when(jnp.logical_and(j == 0, live > SB_LOG_CUTOFF))
    def _():
        block(0, False)

    o_ref[...] = _fold_heads(acc_ref[0], acc_ref[1], first)


def _strict_upper(n):
    r = jnp.arange(n)
    return (r[:, None] > r[None, :]).astype(BF16)


def _sb_attention(q, k, v, batch, seq):
    t = SB_TILE
    assert seq % t == 0
    nq = seq // t
    n_pairs = N_SB_HEADS // HEADS_PER_VREG
    u = _strict_upper(t)
    return pl.pallas_call(
        _sb_kernel,
        grid=(batch, n_pairs, nq),
        in_specs=[
            pl.BlockSpec((t, LANES), lambda b, hp, i: (b * nq + i, hp)),
            pl.BlockSpec((seq, LANES), lambda b, hp, i: (b, hp)),
            pl.BlockSpec((seq, LANES), lambda b, hp, i: (b, hp)),
            pl.BlockSpec((2 * t, t), lambda b, hp, i: (0, 0)),
        ],
        out_specs=pl.BlockSpec((t, LANES), lambda b, hp, i: (b * nq + i, hp)),
        out_shape=jax.ShapeDtypeStruct((batch * seq, SB_WIDTH), F32),
        scratch_shapes=[
            pltpu.VMEM((HEADS_PER_VREG, seq, 2 * LANES), BF16),
            pltpu.VMEM((HEADS_PER_VREG, nq, 2 * t, LANES), BF16),
            pltpu.VMEM((HEADS_PER_VREG, t, LANES), F32),
            pltpu.VMEM((HEADS_PER_VREG, t, 1), F32),
        ],
        compiler_params=_cparams(("parallel", "arbitrary", "arbitrary")),
        name="sb_attention",
    )(q, k, v, jnp.concatenate([u, u], axis=0))


def _top_blocks(gate, pos, allowed, lowest_pos_wins, axis):
    g = jnp.where(allowed, gate, -jnp.inf)
    sel = jnp.zeros(gate.shape, F32)
    for _ in range(MOBA_TOPK):
        m = jnp.max(g, axis=axis, keepdims=True)
        hit = g == m
        if lowest_pos_wins:
            idx = jnp.min(jnp.where(hit, pos, jnp.int32(2 ** 30)), axis=axis, keepdims=True)
        else:
            idx = jnp.max(jnp.where(hit, pos, jnp.int32(-1)), axis=axis, keepdims=True)
        at = pos == idx
        sel = jnp.where(jnp.logical_and(at, m > -jnp.inf), 1.0, sel)
        g = jnp.where(at, -jnp.inf, g)
    return sel


def _moba_kernel(q_ref, k_ref, v_ref, o_ref, kmean_ref, kcat_ref, vcat_ref, m_ref, l_ref, acc_ref, *, n_blocks):
    t = MOBA_BLOCK
    i = pl.program_id(2)
    lane = lax.broadcasted_iota(jnp.int32, (t, LANES), 1)
    first = lane < HEAD_DIM

    @pl.when(i == 0)
    def _():
        kmean_ref[...] = jnp.zeros_like(kmean_ref)

        def build(n, c):
            rows = pl.ds(pl.multiple_of(n * t, t), t)
            k = k_ref[rows, :]
            _put_row(kmean_ref, n, jnp.mean(k, axis=0, keepdims=True))
            for hh, kc in enumerate(_stack_keys(k, first)):
                kcat_ref[hh, rows, :] = kc
            for hh, (hi_lo, _) in enumerate(_stack_values(v_ref[rows, :], first)):
                vcat_ref[hh, rows, :] = hi_lo.astype(BF16)
            return c
        lax.fori_loop(0, n_blocks, build, 0)

    q = q_ref[...]
    q_heads = _stack_queries(q * ATT_SCALE, first)
    q_f32 = (jnp.where(first, q, 0.0), jnp.where(first, 0.0, q))
    kmean = kmean_ref[...]
    bit = jnp.left_shift(1, jnp.minimum(lane, MAX_MOBA_BLOCKS)).astype(F32)
    picked = []
    for hh in range(HEADS_PER_VREG):
        sel = _top_blocks(_dot_f32(q_f32[hh], kmean, nt=True), lane, lane < i, True, 1)
        picked.append(jnp.sum(sel * bit, axis=-1, keepdims=True).astype(jnp.int32))

    def visit(j, keep, n_visit=1):
        rows = pl.ds(pl.multiple_of(j * t, t), n_visit * t)
        for hh in range(HEADS_PER_VREG):
            s = jnp.where(keep[hh], _dot_nt(q_heads[hh], kcat_ref[hh, rows, :]), NEG_BIG)
            m = m_ref[hh]
            m_new = jnp.maximum(m, jnp.max(s, axis=-1, keepdims=True))
            alpha = jnp.exp(m - m_new)
            p = jnp.exp(s - m_new)
            m_ref[hh] = m_new
            l_ref[hh] = alpha * l_ref[hh] + jnp.sum(p, axis=-1, keepdims=True)
            p_hi, p_lo = _split(p, 2)
            vc = vcat_ref[hh, rows, :]
            acc_ref[hh] = alpha * acc_ref[hh] + (_dot(p_hi, vc) + _dot(p_lo, vc))

    m_ref[...] = jnp.full(m_ref.shape, NEG_BIG, F32)
    l_ref[...] = jnp.zeros_like(l_ref)
    acc_ref[...] = jnp.zeros_like(acc_ref)
    causal = (lax.broadcasted_iota(jnp.int32, (t, t), 1) <= lax.broadcasted_iota(jnp.int32, (t, t), 0))
    visit(i, (causal, causal))

    def took(hh, j):
        return jnp.broadcast_to(jnp.bitwise_and(jnp.right_shift(picked[hh], j), 1) > 0, (t, t))

    def body(jj, c):
        j = 2 * jj
        visit(j, [jnp.concatenate([took(hh, j), took(hh, j + 1)], axis=1) for hh in range(HEADS_PER_VREG)], 2)
        return c

    lax.fori_loop(0, i // 2, body, 0)

    @pl.when(i % 2 == 1)
    def _():
        visit(i - 1, [took(hh, i - 1) for hh in range(HEADS_PER_VREG)])
    o_ref[...] = _fold_heads(acc_ref[0] / l_ref[0], acc_ref[1] / l_ref[1], first)


def _moba_attention(q, k, v, batch, seq):
    t = MOBA_BLOCK
    assert seq % t == 0 and seq // t <= MAX_MOBA_BLOCKS
    nq = seq // t
    n_pairs = N_MOBA_HEADS // HEADS_PER_VREG
    first_pair = N_SB_HEADS // HEADS_PER_VREG
    q_map = lambda b, hp, i: (b * nq + i, first_pair + hp)
    kv_map = lambda b, hp, i: (b, first_pair + hp)
    return pl.pallas_call(
        functools.partial(_moba_kernel, n_blocks=nq),
        grid=(batch, n_pairs, nq),
        in_specs=[
            pl.BlockSpec((t, LANES), q_map),
            pl.BlockSpec((seq, LANES), kv_map),
            pl.BlockSpec((seq, LANES), kv_map),
        ],
        out_specs=pl.BlockSpec((t, LANES), lambda b, hp, i: (b * nq + i, hp)),
        out_shape=jax.ShapeDtypeStruct((batch * seq, MOBA_WIDTH), F32),
        scratch_shapes=[
            pltpu.VMEM((LANES, LANES), F32),
            pltpu.VMEM((HEADS_PER_VREG, seq, 2 * LANES), BF16),
            pltpu.VMEM((HEADS_PER_VREG, seq, LANES), BF16),
            pltpu.VMEM((HEADS_PER_VREG, t, 1), F32),
            pltpu.VMEM((HEADS_PER_VREG, t, 1), F32),
            pltpu.VMEM((HEADS_PER_VREG, t, LANES), F32),
        ],
        compiler_params=_cparams(("parallel", "arbitrary", "arbitrary")),
        name="moba_attention",
    )(q, k, v)


def _put_row(ref, p, row):
    base = pl.multiple_of((p // SUBLANES) * SUBLANES, SUBLANES)
    cur = ref[pl.ds(base, SUBLANES), :]
    sub = lax.broadcasted_iota(jnp.int32, cur.shape, 0)
    ref[pl.ds(base, SUBLANES), :] = jnp.where(sub == p % SUBLANES, row, cur)


def _put_col(ref, p, col):
    cur = ref[...]
    lane = lax.broadcasted_iota(jnp.int32, cur.shape, 1)
    ref[...] = jnp.where(lane == p, col, cur)


SAMPLE_BLOCKS_PER_STEP = 2


def _sample_attn_kernel(pt_ref, qrow_ref, knew_ref, vnew_ref, *rest, n_blocks, n_new):
    del pt_ref
    n_page_refs = 2 * SAMPLE_BLOCKS_PER_STEP
    k_pages, v_pages = rest[:n_page_refs], rest[n_page_refs:2 * n_page_refs]
    u_ref, o_ref, carry_ref, osb_ref, m_ref, l_ref, ks_ref, omb_ref = rest[2 * n_page_refs:]
    step = pl.program_id(1)
    p = step
    group_rows = n_new * N_SB_HEADS
    n_rows = 2 * group_rows

    @pl.when(p == 0)
    def _():
        carry_ref[...] = jnp.zeros_like(carry_ref)
        osb_ref[...] = jnp.zeros_like(osb_ref)
        m_ref[...] = jnp.full(m_ref.shape, NEG_BIG, F32)
        l_ref[...] = jnp.zeros_like(l_ref)
        ks_ref[...] = jnp.zeros_like(ks_ref)
        omb_ref[...] = jnp.zeros_like(omb_ref)

    head_diag = (lax.broadcasted_iota(jnp.int32, (N_MOBA_HEADS, MOBA_WIDTH), 1) // HEAD_DIM
                 == lax.broadcasted_iota(jnp.int32, (N_MOBA_HEADS, MOBA_WIDTH), 0)).astype(F32)

    def process(kt, vt, fresh, p):
        keys = kt.shape[1]
        z = _dot((qrow_ref[0] * ATT_SCALE).astype(BF16), kt.astype(BF16))
        row = lax.broadcasted_iota(jnp.int32, (n_rows, keys), 0)
        sp = _softplus(z)
        log_keep = -sp
        s_mb = z
        if fresh:
            key = lax.broadcasted_iota(jnp.int32, (n_rows, keys), 1)
            tok = (row % group_rows) // N_SB_HEADS
            ok_sb = key < tok
            log_keep = jnp.where(ok_sb, log_keep, 0.0)
            s_mb = jnp.where(key <= tok, z, NEG_BIG)
        u = u_ref[0:keys, 0:keys]
        carry = carry_ref[...]
        log_between = _dot_split_lhs(log_keep, u, 2) + carry
        w_sb = jnp.exp(z - sp + log_between)
        if fresh:
            w_sb = jnp.where(ok_sb, w_sb, 0.0)
        carry_ref[...] = carry + jnp.sum(log_keep, axis=1, keepdims=True)
        m_blk = jnp.max(s_mb, axis=1, keepdims=True)
        e = jnp.exp(s_mb - m_blk)
        _put_col(m_ref, p, m_blk)
        _put_col(l_ref, p, jnp.sum(e, axis=1, keepdims=True))
        w = jnp.where(row < group_rows, w_sb, e).astype(BF16)
        vb = vt.astype(BF16)
        osb_ref[...] += _dot_nt(w[0:group_rows], vb[0:SB_WIDTH])
        o_blk = _dot_nt(w[group_rows:n_rows], vb[SB_WIDTH:])
        for tkn in range(n_new):
            part = o_blk[tkn * N_MOBA_HEADS:(tkn + 1) * N_MOBA_HEADS] * head_diag
            _put_row(omb_ref.at[tkn], p, jnp.sum(part, axis=0, keepdims=True))
        _put_col(ks_ref, p, jnp.sum(kt[SB_WIDTH:], axis=1, keepdims=True))

    @pl.when(step == 0)
    def _():
        process(knew_ref[0], vnew_ref[0], True, 0)

    @pl.when(step > 0)
    def _():
        for h in range(SAMPLE_BLOCKS_PER_STEP):
            process(jnp.concatenate([k_pages[2 * h][0], k_pages[2 * h + 1][0]], axis=1),
                    jnp.concatenate([v_pages[2 * h][0], v_pages[2 * h + 1][0]], axis=1), False,
                    (step - 1) * SAMPLE_BLOCKS_PER_STEP + h + 1)

    @pl.when(step == n_blocks // SAMPLE_BLOCKS_PER_STEP)
    def _():
        lane = lax.broadcasted_iota(jnp.int32, (n_rows, LANES), 1)
        gate = _dot_f32(qrow_ref[0][:, SB_WIDTH:], ks_ref[...])
        past = jnp.logical_and(lane >= 1, lane <= n_blocks)
        sel = _top_blocks(gate, lane, past, False, 1)
        sel = jnp.logical_or(sel > 0.5, lane == 0)
        m_all = m_ref[...]
        m_top = jnp.max(jnp.where(sel, m_all, NEG_BIG), axis=1, keepdims=True)
        coef = jnp.where(sel, jnp.exp(m_all - m_top), 0.0)
        coef = coef / jnp.sum(coef * l_ref[...], axis=1, keepdims=True)
        coef_t = jnp.concatenate([coef, jnp.zeros((LANES - n_rows, LANES), F32)], axis=0).T
        coef_t = coef_t[0:omb_ref.shape[1]]
        col_head = lax.broadcasted_iota(jnp.int32, (LANES, MOBA_WIDTH), 1) // HEAD_DIM
        src_row = lax.broadcasted_iota(jnp.int32, (LANES, MOBA_WIDTH), 0)
        osb = osb_ref[...]
        for tkn in range(n_new):
            spread = (src_row == group_rows + tkn * N_MOBA_HEADS + col_head).astype(BF16)
            weighted = _dot_split_lhs(coef_t, spread, 3) * omb_ref[tkn]
            o_ref[0, tkn:tkn + 1, SB_WIDTH:] = jnp.sum(weighted, axis=0, keepdims=True)
            part = osb[tkn * N_SB_HEADS:(tkn + 1) * N_SB_HEADS] * head_diag
            o_ref[0, tkn:tkn + 1, 0:SB_WIDTH] = jnp.sum(part, axis=0, keepdims=True)


def _sample_attention(q, k_new, v_new, cache_kt, cache_vt, page_table):
    n_seq, n_new, _ = q.shape
    _, _, page = cache_kt.shape
    n_pages = page_table.shape[1]
    assert MOBA_BLOCK == 2 * page and page == LANES and n_pages % 2 == 0 and n_new <= SUBLANES
    assert N_SB_HEADS == N_MOBA_HEADS and 2 * n_new * N_SB_HEADS <= LANES
    n_blocks = n_pages // 2
    per_step = SAMPLE_BLOCKS_PER_STEP
    assert n_blocks % per_step == 0 and n_blocks + 1 <= LANES
    steps = n_blocks // per_step + 1
    steps_pad = -(-(n_blocks + 1) // BF16_ROWS) * BF16_ROWS
    n_rows = 2 * n_new * N_SB_HEADS

    qh = q.reshape(n_seq, n_new, 2, N_SB_HEADS, HEAD_DIM)
    eye_g = jnp.eye(2, dtype=F32)
    eye_h = jnp.eye(N_SB_HEADS, dtype=F32)
    qrow = jnp.einsum("btghd,gG,hH->bGtHghd", qh, eye_g, eye_h).reshape(n_seq, n_rows, ATT_WIDTH)
    new_t = lambda a: jnp.pad(jnp.swapaxes(a, 1, 2), ((0, 0), (0, 0), (0, page - n_new)))

    def page_map(which):
        h, half = divmod(which, 2)
        def index(b, p, pt):
            blk = n_blocks - 1 - ((jnp.maximum(p, 1) - 1) * per_step + h)
            return (pt[b * n_pages + 2 * blk + half], 0, 0)
        return index

    seq_map = lambda b, p, pt: (b, 0, 0)
    page_specs = [pl.BlockSpec((1, ATT_WIDTH, page), page_map(which)) for which in range(2 * per_step)]
    grid_spec = pltpu.PrefetchScalarGridSpec(
        num_scalar_prefetch=1,
        grid=(n_seq, steps),
        in_specs=[
            pl.BlockSpec((1, n_rows, ATT_WIDTH), seq_map),
            pl.BlockSpec((1, ATT_WIDTH, page), seq_map),
            pl.BlockSpec((1, ATT_WIDTH, page), seq_map),
            *page_specs, *page_specs,
            pl.BlockSpec((MOBA_BLOCK, MOBA_BLOCK), lambda b, p, pt: (0, 0)),
        ],
        out_specs=pl.BlockSpec((1, n_new, ATT_WIDTH), seq_map),
        scratch_shapes=[
            pltpu.VMEM((n_rows, 1), F32),
            pltpu.VMEM((n_new * N_SB_HEADS, SB_WIDTH), F32),
            pltpu.VMEM((n_rows, LANES), F32),
            pltpu.VMEM((n_rows, LANES), F32),
            pltpu.VMEM((MOBA_WIDTH, LANES), F32),
            pltpu.VMEM((n_new, steps_pad, MOBA_WIDTH), F32),
        ],
    )
    return pl.pallas_call(
        functools.partial(_sample_attn_kernel, n_blocks=n_blocks, n_new=n_new),
        grid_spec=grid_spec,
        out_shape=jax.ShapeDtypeStruct((n_seq, n_new, ATT_WIDTH), F32),
        compiler_params=_cparams(("parallel", "arbitrary")),
        name="sample_attention",
    )(page_table.reshape(-1), qrow, new_t(k_new), new_t(v_new),
      *([cache_kt] * (2 * per_step)), *([cache_vt] * (2 * per_step)), _strict_upper(MOBA_BLOCK))


def _attn_out_kernel(x_ref, osb_ref, omb_ref, wsb_hi_ref, wsb_lo_ref, wmb_hi_ref, wmb_lo_ref, y_ref):
    sb = _dot3(_split(osb_ref[...], 2), (wsb_hi_ref[...], wsb_lo_ref[...]))
    mb = _dot3(_split(omb_ref[...], 2), (wmb_hi_ref[...], wmb_lo_ref[...]))
    y_ref[...] = x_ref[...] + (sb + mb)


def _attn_out(x2d, o_sb, o_mb, w_o_parts):
    n, d = x2d.shape
    tm = _row_tile(n, 512)
    row = lambda i: (i, 0)
    fixed = lambda i: (0, 0)
    return pl.pallas_call(
        _attn_out_kernel,
        grid=(n // tm,),
        in_specs=[
            pl.BlockSpec((tm, d), row),
            pl.BlockSpec((tm, SB_WIDTH), row),
            pl.BlockSpec((tm, MOBA_WIDTH), row),
            pl.BlockSpec((SB_WIDTH, d), fixed),
            pl.BlockSpec((SB_WIDTH, d), fixed),
            pl.BlockSpec((MOBA_WIDTH, d), lambda i: (1, 0)),
            pl.BlockSpec((MOBA_WIDTH, d), lambda i: (1, 0)),
        ],
        out_specs=pl.BlockSpec((tm, d), row),
        out_shape=jax.ShapeDtypeStruct((n, d), F32),
        compiler_params=_cparams(("parallel",)),
        name="attn_out_proj",
    )(x2d, o_sb, o_mb, w_o_parts[0], w_o_parts[1], w_o_parts[0], w_o_parts[1])


def _pool_kernel(*refs, pos0, rows, has_prev):
    if has_prev:
        x_ref, xprev_ref, pre_ref, g_ref, w_ref, sc_ref, y_ref, tail_ref = refs
    else:
        x_ref, pre_ref, g_ref, w_ref, sc_ref, y_ref, tail_ref = refs
    ti = pl.program_id(1)
    g = g_ref[...]
    x = x_ref[0]
    d = x.shape[-1]
    h = _rms(x, g)
    halo = pre_ref[0]
    if has_prev:
        halo = jnp.where(ti == 0, halo, _rms(xprev_ref[0], g))
    pad = (-rows) % SUBLANES
    pieces = [halo, h] + ([jnp.zeros((pad, d), F32)] if pad else [])
    full = jnp.concatenate(pieces, axis=0)
    tail_ref[0] = full[rows:rows + POOL_HALO]

    group = d // len(POOL_WINDOWS)
    pos = pos0 + ti * rows + lax.broadcasted_iota(jnp.int32, (rows, 1), 0)
    s = full
    ys = []
    for gi, win in enumerate(POOL_WINDOWS):
        s = s[:, (group if gi else 0):]
        s = s + pltpu.roll(s, win // 2, 0)
        count = jnp.minimum(win, pos + 1).astype(F32)
        diff = s[POOL_HALO:POOL_HALO + rows, :group] / count - h[:, gi * group:(gi + 1) * group]
        ys.append(_dot(diff.astype(BF16), w_ref[gi]))
    y_ref[0] = x + jnp.concatenate(ys, axis=-1) * sc_ref[...]


def _pool_mixer(x, prefix_h, g, w_pool_bf16, scale, pos0):
    b, t, d = x.shape
    rows = min(t, 512)
    assert t % rows == 0 and rows % POOL_HALO == 0 or t == rows
    has_prev = t > rows
    nt = t // rows
    per = rows // POOL_HALO if has_prev else 1
    in_specs = [pl.BlockSpec((1, rows, d), lambda bi, ti: (bi, ti, 0))]
    args = [x]
    if has_prev:
        in_specs.append(pl.BlockSpec((1, POOL_HALO, d), lambda bi, ti: (bi, jnp.maximum(ti * per - 1, 0), 0)))
        args.append(x)
    fixed2 = lambda bi, ti: (0, 0)
    in_specs += [
        pl.BlockSpec((1, POOL_HALO, d), lambda bi, ti: (bi, 0, 0)),
        pl.BlockSpec((1, d), fixed2),
        pl.BlockSpec(w_pool_bf16.shape, lambda bi, ti: (0, 0, 0)),
        pl.BlockSpec((1, d), fixed2),
    ]
    args += [prefix_h, g.reshape(1, d), w_pool_bf16, scale.reshape(1, d)]
    return pl.pallas_call(
        functools.partial(_pool_kernel, pos0=pos0, rows=rows, has_prev=has_prev),
        grid=(b, nt),
        in_specs=in_specs,
        out_specs=[
            pl.BlockSpec((1, rows, d), lambda bi, ti: (bi, ti, 0)),
            pl.BlockSpec((1, POOL_HALO, d), lambda bi, ti: (bi, 0, 0)),
        ],
        out_shape=[jax.ShapeDtypeStruct((b, t, d), F32), jax.ShapeDtypeStruct((b, POOL_HALO, d), F32)],
        compiler_params=_cparams(("parallel", "arbitrary")),
        name="pool_mixer",
    )(*args)


def _router_kernel(x_ref, g_ref, wr_ref, xn_ref, comb_ref, route_ref):
    xn = _rms(x_ref[...], g_ref[...])
    xn_ref[...] = xn.astype(xn_ref.dtype)
    logits = _dot_f32(xn, wr_ref[...])
    lane = lax.broadcasted_iota(jnp.int32, logits.shape, 1)
    is_group = jnp.logical_and(lane >= N_EXPERTS, lane < N_EXPERTS + N_EXPERT_GROUPS)
    glog = jnp.where(is_group, logits, -jnp.inf)
    gmax = jnp.max(glog, axis=-1, keepdims=True)
    gsum = jnp.sum(jnp.exp(glog - gmax), axis=-1, keepdims=True)
    g_val = 1.0 / gsum
    g_idx = jnp.min(jnp.where(glog == gmax, lane, 2 * LANES), axis=-1, keepdims=True) - N_EXPERTS
    in_group = jnp.logical_and(lane >= g_idx * EXPERTS_PER_GROUP, lane < (g_idx + 1) * EXPERTS_PER_GROUP)
    elog = jnp.where(in_group, logits, -jnp.inf)
    eexp = jnp.exp(elog - jnp.max(elog, axis=-1, keepdims=True))
    prob = eexp / jnp.sum(eexp, axis=-1, keepdims=True)
    p1 = jnp.max(prob, axis=-1, keepdims=True)
    i1 = jnp.min(jnp.where(jnp.logical_and(in_group, prob == p1), lane, 2 * LANES), axis=-1, keepdims=True)
    rest = jnp.where(jnp.logical_and(in_group, lane != i1), prob, -1.0)
    p2 = jnp.max(rest, axis=-1, keepdims=True)
    i2 = jnp.min(jnp.where(rest == p2, lane, 2 * LANES), axis=-1, keepdims=True)
    norm = g_val / (p1 + p2)
    comb_ref[...] = jnp.where(lane == i1, p1 * norm, 0.0) + jnp.where(lane == i2, p2 * norm, 0.0)
    route_ref[...] = (jnp.where(lane == 0, p1 * norm, 0.0) + jnp.where(lane == 1, p2 * norm, 0.0)
                      + jnp.where(lane == 2, i1.astype(F32), 0.0) + jnp.where(lane == 3, i2.astype(F32), 0.0))


def _router(x2d, g, w_router, xn_dtype):
    n, d = x2d.shape
    tm = _row_tile(n, 512)
    row = lambda i: (i, 0)
    fixed = lambda i: (0, 0)
    return pl.pallas_call(
        _router_kernel,
        grid=(n // tm,),
        in_specs=[pl.BlockSpec((tm, d), row), pl.BlockSpec((1, d), fixed), pl.BlockSpec((d, LANES), fixed)],
        out_specs=[pl.BlockSpec((tm, d), row), pl.BlockSpec((tm, LANES), row), pl.BlockSpec((tm, LANES), row)],
        out_shape=[jax.ShapeDtypeStruct((n, d), xn_dtype), jax.ShapeDtypeStruct((n, LANES), F32),
                   jax.ShapeDtypeStruct((n, LANES), F32)],
        compiler_params=_cparams(("parallel",)),
        name="ffn_router",
    )(x2d, g.reshape(1, d), w_router)


MOE_TILE = 256
GATHER_UNROLL = 8


def _dispatch_plan(route, n_exp, tm):
    n = route.shape[0]
    ids = route[:, 2:4].astype(jnp.int32)
    e = jnp.concatenate([ids[:, 0], ids[:, 1]])
    onehot = (e[:, None] == jnp.arange(n_exp, dtype=jnp.int32)[None, :]).astype(jnp.int32)
    seen = jnp.cumsum(onehot, axis=0)
    counts = seen[-1]
    rank = jnp.sum(onehot * seen, axis=1) - 1
    padded = ((counts + tm - 1) // tm) * tm
    group_start = jnp.cumsum(padded) - padded
    dest = group_start[e] + rank
    n_rows = 2 * n + n_exp * tm
    tok = jnp.concatenate([jnp.arange(n, dtype=jnp.int32)] * 2)
    row_token = jnp.zeros((n_rows,), jnp.int32).at[dest].set(tok, unique_indices=True)
    n_tiles = n_rows // tm
    used_tiles = (jnp.sum(padded) // tm).astype(jnp.int32)
    tile_start = jnp.arange(n_tiles, dtype=jnp.int32) * tm
    group_end = group_start + padded
    tile_expert = jnp.sum((tile_start[:, None] >= group_end[None, :]).astype(jnp.int32), axis=1)
    last = tile_expert[jnp.maximum(used_tiles - 1, 0)]
    tile_expert = jnp.where(jnp.arange(n_tiles) < used_tiles, tile_expert, last).astype(jnp.int32)
    return row_token, tile_expert, used_tiles.reshape(1), dest[:n], dest[n:]


def _expert_rows_kernel(te_ref, rt_ref, used_ref, x_hbm, wg_ref, wu_ref, wd_ref, y_ref, xbuf, sem):
    del te_ref
    tm = y_ref.shape[0]
    i = pl.program_id(0)
    used = used_ref[0]
    slot = i % 2

    def row_copy(tile, r, slot_):
        tok = rt_ref[tile * tm + r]
        return pltpu.make_async_copy(x_hbm.at[pl.ds(tok, 1), :], xbuf.at[slot_, pl.ds(r, 1), :], sem.at[slot_])

    def start_gather(tile, slot_):
        def trip(c, carry):
            for k in range(GATHER_UNROLL):
                row_copy(tile, c * GATHER_UNROLL + k, slot_).start()
            return carry
        lax.fori_loop(0, tm // GATHER_UNROLL, trip, 0)

    @pl.when(i == 0)
    def _():
        start_gather(0, 0)

    @pl.when(i + 1 < used)
    def _():
        start_gather(i + 1, 1 - slot)

    @pl.when(i < used)
    def _():
        pltpu.make_async_copy(xbuf.at[slot], xbuf.at[slot], sem.at[slot]).wait()
        x = xbuf[slot].astype(BF16)
        gate = _dot(x, wg_ref[0])
        up = _dot(x, wu_ref[0])
        hidden = gate * jax.nn.sigmoid(gate) * up
        y_ref[...] = _dot(hidden.astype(BF16), wd_ref[0])

    @pl.when(i >= used)
    def _():
        y_ref[...] = jnp.zeros_like(y_ref)


def _expert_rows(xn, row_token, tile_expert, used_tiles, wg, wu, wd):
    n, d = xn.shape
    n_exp, _, ff = wg.shape
    tm = MOE_TILE
    n_rows = row_token.shape[0]
    w_map = lambda i, te, rt, used: (te[i], 0, 0)
    grid_spec = pltpu.PrefetchScalarGridSpec(
        num_scalar_prefetch=3,
        grid=(n_rows // tm,),
        in_specs=[
            pl.BlockSpec(memory_space=pl.ANY),
            pl.BlockSpec((1, d, ff), w_map),
            pl.BlockSpec((1, d, ff), w_map),
            pl.BlockSpec((1, ff, d), w_map),
        ],
        out_specs=pl.BlockSpec((tm, d), lambda i, te, rt, used: (i, 0)),
        scratch_shapes=[pltpu.VMEM((2, tm, d), F32), pltpu.SemaphoreType.DMA((2,))],
    )
    return pl.pallas_call(
        _expert_rows_kernel,
        grid_spec=grid_spec,
        out_shape=jax.ShapeDtypeStruct((n_rows, d), F32),
        compiler_params=_cparams(("arbitrary",)),
        name="moe_expert_rows",
    )(tile_expert, row_token, used_tiles, xn, wg, wu, wd)


def _moe_mix_kernel(p1_ref, p2_ref, x_ref, route_ref, ys_hbm, o_ref, buf, sem):
    tc = o_ref.shape[0]
    i = pl.program_id(0)
    n = pl.num_programs(0)
    slot = i % 2

    def start_gather(tile, slot_):
        def trip(c, carry):
            for k in range(GATHER_UNROLL):
                r = c * GATHER_UNROLL + k
                for which, pos_ref in enumerate((p1_ref, p2_ref)):
                    pltpu.make_async_copy(ys_hbm.at[pl.ds(pos_ref[tile * tc + r], 1), :],
                                          buf.at[slot_, which, pl.ds(r, 1), :], sem.at[slot_]).start()
            return carry
        lax.fori_loop(0, tc // GATHER_UNROLL, trip, 0)

    @pl.when(i == 0)
    def _():
        start_gather(0, 0)

    @pl.when(i + 1 < n)
    def _():
        start_gather(i + 1, 1 - slot)

    pltpu.make_async_copy(buf.at[slot], buf.at[slot], sem.at[slot]).wait()
    route = route_ref[...]
    o_ref[...] = x_ref[...] + (route[:, 0:1] * buf[slot, 0] + route[:, 1:2] * buf[slot, 1])


def _moe_mix(x2d, route, ys, pos1, pos2):
    n, d = x2d.shape
    tc = MOE_TILE
    row = lambda i, p1, p2: (i, 0)
    grid_spec = pltpu.PrefetchScalarGridSpec(
        num_scalar_prefetch=2,
        grid=(n // tc,),
        in_specs=[pl.BlockSpec((tc, d), row), pl.BlockSpec((tc, LANES), row), pl.BlockSpec(memory_space=pl.ANY)],
        out_specs=pl.BlockSpec((tc, d), row),
        scratch_shapes=[pltpu.VMEM((2, 2, tc, d), F32), pltpu.SemaphoreType.DMA((2,))],
    )
    return pl.pallas_call(
        _moe_mix_kernel,
        grid_spec=grid_spec,
        out_shape=jax.ShapeDtypeStruct((n, d), F32),
        compiler_params=_cparams(("arbitrary",)),
        name="moe_mix",
    )(pos1, pos2, x2d, route, ys)


def _moe_kernel(x_ref, xn_ref, comb_ref, wg_ref, wu_ref, wd_ref, y_ref):
    e = pl.program_id(1)

    @pl.when(e == 0)
    def _():
        y_ref[...] = x_ref[...]

    xn = xn_ref[...]
    gate = _dot(xn, wg_ref[0])
    up = _dot(xn, wu_ref[0])
    ff = gate.shape[-1]
    pick = (lax.broadcasted_iota(jnp.int32, (LANES, ff), 0) == e).astype(BF16)
    weight = _dot_split_lhs(comb_ref[...], pick, 3)
    hidden = gate * jax.nn.sigmoid(gate) * up * weight
    y_ref[...] += _dot(hidden.astype(BF16), wd_ref[0])


def _moe(x2d, xn, comb, wg, wu, wd):
    n, d = x2d.shape
    n_exp, _, ff = wg.shape
    tm = _row_tile(n, 1024)
    row = lambda i, e: (i, 0)
    return pl.pallas_call(
        _moe_kernel,
        grid=(n // tm, n_exp),
        in_specs=[
            pl.BlockSpec((tm, d), row),
            pl.BlockSpec((tm, d), row),
            pl.BlockSpec((tm, LANES), row),
            pl.BlockSpec((1, d, ff), lambda i, e: (e, 0, 0)),
            pl.BlockSpec((1, d, ff), lambda i, e: (e, 0, 0)),
            pl.BlockSpec((1, ff, d), lambda i, e: (e, 0, 0)),
        ],
        out_specs=pl.BlockSpec((tm, d), row),
        out_shape=jax.ShapeDtypeStruct((n, d), F32),
        compiler_params=_cparams(("parallel", "arbitrary")),
        name="moe_experts",
    )(x2d, xn, comb, wg, wu, wd)


def _router_weight(w_group, w_expert):
    d = w_group.shape[0]
    w = jnp.concatenate([w_expert, w_group], axis=1)
    return jnp.pad(w, ((0, 0), (0, LANES - w.shape[1]))).astype(F32).reshape(d, LANES)


def _cache_pages_t(cache):
    layers, pages, page, heads, dim = cache.shape
    return jnp.transpose(cache, (0, 1, 3, 4, 2)).reshape(layers * pages, heads * dim, page)


def _ffn(x2d, g, w_router, wg, wu, wd):
    n = x2d.shape[0]
    n_exp = wg.shape[0]
    grouped_rows = 2 * n + n_exp * MOE_TILE
    if n % MOE_TILE == 0 and n * n_exp > 4 * grouped_rows:
        xn, _, route = _router(x2d, g, w_router, F32)
        row_token, tile_expert, used_tiles, pos1, pos2 = _dispatch_plan(route, n_exp, MOE_TILE)
        ys = _expert_rows(xn, row_token, tile_expert, used_tiles, wg, wu, wd)
        return _moe_mix(x2d, route, ys, pos1, pos2)
    xn, comb, _ = _router(x2d, g, w_router, BF16)
    return _moe(x2d, xn, comb, wg, wu, wd)


def kernel(x_prompt, x_sample, cache_k, cache_v, state_pool, page_table, norm_mix_g, norm_ffn_g, att_w_qkv, att_q_norm_g, att_k_norm_g, att_w_o, pool_w, pool_scale, moe_w_router_group, moe_w_router_expert, moe_w_gate, moe_w_up, moe_w_down):
    batch, seq, d = x_prompt.shape
    n_seq, n_new, _ = x_sample.shape
    depth = norm_mix_g.shape[0]
    page = cache_k.shape[2]
    n_past = page_table.shape[1] * page
    assert depth == 2 and d == ATT_WIDTH and n_past % MOBA_BLOCK == 0

    w_qkv = _split_weights(att_w_qkv[0])
    w_o = _split_weights(att_w_o[0])
    w_pool = pool_w.astype(BF16)
    wg, wu, wd = moe_w_gate.astype(BF16), moe_w_up.astype(BF16), moe_w_down.astype(BF16)
    w_router = [_router_weight(moe_w_router_group[l], moe_w_router_expert[l]) for l in range(depth)]
    ffn = lambda x2d, l: _ffn(x2d, norm_ffn_g[l], w_router[l], wg[l], wu[l], wd[l])

    xp = x_prompt.reshape(batch * seq, d)
    xs = x_sample.reshape(n_seq * n_new, d)

    qg, kg = att_q_norm_g[0].reshape(-1), att_k_norm_g[0].reshape(-1)
    q_p, k_p, v_p = _qkv(xp, norm_mix_g[0], w_qkv, qg, kg)
    o_sb = _sb_attention(q_p, k_p, v_p, batch, seq)
    o_mb = _moba_attention(q_p, k_p, v_p, batch, seq)
    xp = _attn_out(xp, o_sb, o_mb, w_o)

    q_s, k_s, v_s = _qkv(xs, norm_mix_g[0], w_qkv, qg, kg)
    o_s = _sample_attention(
        q_s.reshape(n_seq, n_new, d), k_s.reshape(n_seq, n_new, d), v_s.reshape(n_seq, n_new, d),
        _cache_pages_t(cache_k), _cache_pages_t(cache_v), page_table)
    o_s = o_s.reshape(n_seq * n_new, d)
    xs = _attn_out(xs, o_s[:, :SB_WIDTH], o_s[:, SB_WIDTH:], w_o)

    xp = ffn(xp, 0)
    xs = ffn(xs, 0)

    zeros_p = jnp.zeros((batch, POOL_HALO, d), F32)
    xp3, tail_p = _pool_mixer(xp.reshape(batch, seq, d), zeros_p, norm_mix_g[1], w_pool[0], pool_scale[0], 0)
    prefix_s = jnp.pad(state_pool[0], ((0, 0), (POOL_HALO - POOL_STATE, 0), (0, 0)))
    xs3, tail_s = _pool_mixer(xs.reshape(n_seq, n_new, d), prefix_s, norm_mix_g[1], w_pool[0], pool_scale[0], n_past)
    xp = ffn(xp3.reshape(batch * seq, d), 1)
    xs = ffn(xs3.reshape(n_seq * n_new, d), 1)

    heads = (N_ATT_HEADS, HEAD_DIM)
    return (
        xp.reshape(batch, seq, d),
        xs.reshape(n_seq, n_new, d),
        k_p.reshape(1, batch, seq, *heads),
        v_p.reshape(1, batch, seq, *heads),
        k_s.reshape(1, n_seq, n_new, *heads),
        v_s.reshape(1, n_seq, n_new, *heads),
        tail_p[None, :, POOL_HALO - POOL_STATE:],
        tail_s[None, :, POOL_HALO - POOL_STATE:],
    )
```
